```python
import jax, jax.numpy as jnp
from jax import lax
import numpy as np

D_MODEL = 1024
BATCH = 4
SEQ = 8192
DEPTH = 1
DEC_BATCH = 128
DEC_SEQ = 8
PAST_LEN = 8192
PAGE_SIZE = 128

HEAD_DIM = 64
ATT_HEADS = D_MODEL // (2 * HEAD_DIM)
GM_HEAD_DIM = 64
GM_HEADS = D_MODEL // (2 * GM_HEAD_DIM)
ATT_WIDTH = ATT_HEADS * HEAD_DIM
GM_WIDTH = GM_HEADS * GM_HEAD_DIM
MIX_WIDTH = ATT_WIDTH + GM_WIDTH
IN_WIDTH = 3 * ATT_WIDTH + 2 * GM_WIDTH
MOBA_BLOCK = 256
MOBA_TOPK = 3
Q_BLOCK = 32
CHUNK = 128
D_FF = 4 * D_MODEL
NORM_EPS = 1e-6
ATTN_SCALE = HEAD_DIM ** -0.5

kernel_name = "hymba_moba_gmlp_decode_step"


def _rms_norm(x, g):
    xf = x.astype(jnp.float32)
    y = xf * lax.rsqrt(jnp.mean(xf * xf, axis=-1, keepdims=True) + NORM_EPS)
    return (y * g.astype(jnp.float32)).astype(x.dtype)


def _layer_norm(x, g, b):
    xf = x.astype(jnp.float32)
    mu = jnp.mean(xf, axis=-1, keepdims=True)
    var = jnp.mean(jnp.square(xf - mu), axis=-1, keepdims=True)
    y = (xf - mu) * lax.rsqrt(var + NORM_EPS)
    return (y * g.astype(jnp.float32) + b.astype(jnp.float32)).astype(x.dtype)


def _alibi_slopes():
    h = jnp.arange(1, ATT_HEADS + 1, dtype=jnp.float32)
    return jnp.exp2(-8.0 * h / ATT_HEADS)


def _project(x, ln_g, w_in, q_g, k_g, gln_g, gln_b):
    B, T, _ = x.shape
    z = _rms_norm(x, ln_g) @ w_in
    q, k, v, u, g = jnp.split(z, [ATT_WIDTH, 2 * ATT_WIDTH, 3 * ATT_WIDTH,
                                  3 * ATT_WIDTH + GM_WIDTH], axis=-1)
    q = _rms_norm(q.reshape(B, T, ATT_HEADS, HEAD_DIM), q_g)
    k = _rms_norm(k.reshape(B, T, ATT_HEADS, HEAD_DIM), k_g)
    v = v.reshape(B, T, ATT_HEADS, HEAD_DIM)
    u = jax.nn.gelu(u.reshape(B, T, GM_HEADS, GM_HEAD_DIM))
    g = _layer_norm(jax.nn.gelu(g.reshape(B, T, GM_HEADS, GM_HEAD_DIM)), gln_g, gln_b)
    return q, k, v, u, g


def _attend(q, pos_q, parts, slopes):
    scores = []
    for k, _, pos_k, mask in parts:
        eq = 'bhqd,bhkd->bhqk' if k.ndim == 4 else 'bhqd,bhqkd->bhqk'
        s = jnp.einsum(eq, q, k).astype(jnp.float32) * ATTN_SCALE
        dist = (pos_q[:, None] - pos_k).astype(jnp.float32)
        s = s - slopes[:, None, None] * dist
        scores.append(jnp.where(mask, s, -jnp.inf))
    p = jax.nn.softmax(jnp.concatenate(scores, axis=-1), axis=-1).astype(q.dtype)
    outs = []
    off = 0
    for (k, v, _, _), s in zip(parts, scores):
        n = s.shape[-1]
        eq = 'bhqk,bhkd->bhqd' if v.ndim == 4 else 'bhqk,bhqkd->bhqd'
        outs.append(jnp.einsum(eq, p[..., off:off + n], v))
        off += n
    return sum(outs)


def _moba_prompt(q, k, v, slopes):
    B, T, H, Dh = q.shape
    q, k, v = (t.transpose(0, 2, 1, 3) for t in (q, k, v))
    nb = -(-T // MOBA_BLOCK)
    pad = ((0, 0), (0, 0), (0, nb * MOBA_BLOCK - T), (0, 0))
    k_pad, v_pad = jnp.pad(k, pad), jnp.pad(v, pad)
    k_blk = k_pad.reshape(B, H, nb, MOBA_BLOCK, Dh)
    v_blk = v_pad.reshape(B, H, nb, MOBA_BLOCK, Dh)
    k_mean = k_blk.astype(jnp.float32).mean(axis=3).astype(q.dtype)
    n_sel = min(MOBA_TOPK, nb)
    b_idx = jnp.arange(B)[:, None, None, None]
    h_idx = jnp.arange(H)[None, :, None, None]
    blk_ids = jnp.arange(nb)
    offs = jnp.arange(MOBA_BLOCK)

    def query_block(qi):
        q0 = qi * Q_BLOCK
        qb = lax.dynamic_slice_in_dim(q, q0, Q_BLOCK, axis=2)
        pos_q = q0 + jnp.arange(Q_BLOCK)
        cur = q0 // MOBA_BLOCK
        own_k = lax.dynamic_slice_in_dim(k_pad, cur * MOBA_BLOCK, MOBA_BLOCK, axis=2)
        own_v = lax.dynamic_slice_in_dim(v_pad, cur * MOBA_BLOCK, MOBA_BLOCK, axis=2)
        pos_own = cur * MOBA_BLOCK + offs
        own_mask = pos_own[None, :] <= pos_q[:, None]
        gate = jnp.einsum('bhqd,bhnd->bhqn', qb, k_mean).astype(jnp.float32)
        gate = jnp.where(blk_ids < cur, gate, -jnp.inf)
        _, sel = lax.top_k(gate, n_sel)
        k_sel = k_blk[b_idx, h_idx, sel].reshape(B, H, Q_BLOCK, n_sel * MOBA_BLOCK, Dh)
        v_sel = v_blk[b_idx, h_idx, sel].reshape(B, H, Q_BLOCK, n_sel * MOBA_BLOCK, Dh)
        pos_sel = (sel[..., None] * MOBA_BLOCK + offs).reshape(B, H, Q_BLOCK, n_sel * MOBA_BLOCK)
        sel_mask = jnp.repeat(sel < cur, MOBA_BLOCK, axis=-1)
        return _attend(qb, pos_q, [(k_sel, v_sel, pos_sel, sel_mask),
                                   (own_k, own_v, pos_own, own_mask)], slopes)

    out = lax.map(query_block, jnp.arange(T // Q_BLOCK))
    return out.transpose(1, 0, 3, 2, 4).reshape(B, T, H, Dh)


def _moba_sample(q, k_new, v_new, cache_k, cache_v, layer, page_table, slopes):
    Bd, Tn, H, Dh = q.shape
    past = page_table.shape[1] * PAGE_SIZE
    ppb = MOBA_BLOCK // PAGE_SIZE
    q = q.transpose(0, 2, 1, 3)
    pos_q = past + jnp.arange(Tn)
    cur = past // MOBA_BLOCK
    own_start = cur * MOBA_BLOCK
    parts = []
    if cur > 0:
        n_sel = min(MOBA_TOPK, cur)
        k_past = cache_k[layer, page_table[:, :cur * ppb]]
        k_mean = k_past.reshape(Bd, cur, MOBA_BLOCK, H, Dh).astype(jnp.float32).mean(axis=2).astype(q.dtype)
        gate = jnp.einsum('bhqd,bnhd->bhqn', q, k_mean).astype(jnp.float32)
        _, sel = lax.top_k(gate, n_sel)
        lpage = sel[..., None] * ppb + jnp.arange(ppb)
        phys = page_table[jnp.arange(Bd)[:, None, None, None, None], lpage]
        idx = (layer, phys[..., None], jnp.arange(PAGE_SIZE),
               jnp.arange(H)[None, :, None, None, None, None])
        k_sel = cache_k[idx].reshape(Bd, H, Tn, n_sel * MOBA_BLOCK, Dh)
        v_sel = cache_v[idx].reshape(Bd, H, Tn, n_sel * MOBA_BLOCK, Dh)
        pos_sel = (sel[..., None] * MOBA_BLOCK + jnp.arange(MOBA_BLOCK)).reshape(Bd, H, Tn, n_sel * MOBA_BLOCK)
        parts.append((k_sel, v_sel, pos_sel, jnp.ones(pos_sel.shape, dtype=bool)))
    n_own = past - own_start
    if n_own > 0:
        own_pages = page_table[:, own_start // PAGE_SIZE:]
        k_own = cache_k[layer, own_pages].reshape(Bd, n_own, H, Dh).transpose(0, 2, 1, 3)
        v_own = cache_v[layer, own_pages].reshape(Bd, n_own, H, Dh).transpose(0, 2, 1, 3)
        pos_own = own_start + jnp.arange(n_own)
        parts.append((k_own, v_own, pos_own, jnp.ones((Tn, n_own), dtype=bool)))
    parts.append((k_new.transpose(0, 2, 1, 3), v_new.transpose(0, 2, 1, 3), pos_q,
                  pos_q[None, :] <= pos_q[:, None]))
    return _attend(q, pos_q, parts, slopes).transpose(0, 2, 1, 3)


def _sgu_prompt(u, g, w_s, b_s):
    B, T, Hg, Dg = g.shape
    gc = g.reshape(B, T // CHUNK, CHUNK, Hg, Dg)
    mixed = jnp.einsum('hij,bcjhd->bcihd', jnp.tril(w_s), gc) + b_s.T[:, :, None]
    return u * mixed.reshape(B, T, Hg, Dg)


def _sgu_sample(u, g, w_s, b_s):
    Tn = g.shape[1]
    w = jnp.tril(w_s[:, :Tn, :Tn])
    mixed = jnp.einsum('hij,bjhd->bihd', w, g) + b_s[:, :Tn].T[:, :, None]
    return u * mixed


def _merge_and_mlp(x, a, m, att_out_g, gm_out_g, w_out, ln2_g, w_up, w_down):
    B, T, _ = x.shape
    mix = jnp.concatenate([_rms_norm(a.reshape(B, T, ATT_WIDTH), att_out_g),
                           _rms_norm(m.reshape(B, T, GM_WIDTH), gm_out_g)], axis=-1)
    x = x + mix @ w_out
    h = jnp.square(jax.nn.relu(_rms_norm(x, ln2_g) @ w_up))
    return x + h @ w_down


def setup_inputs(seed: int = 0) -> dict:
    key = jax.random.key(seed)
    ks = jax.random.split(key, 20)
    f32 = jnp.float32
    n_pages = PAST_LEN // PAGE_SIZE
    n_used = DEC_BATCH * n_pages
    n_pool = n_used + n_used // 4

    def nrm(k, shape, scale):
        return jax.random.normal(k, shape, f32) * scale

    def gain(k, shape):
        return 1.0 + 0.1 * jax.random.normal(k, shape, f32)

    page_table = jax.random.permutation(ks[4], n_pool)[:n_used].reshape(DEC_BATCH, n_pages).astype(jnp.int32)
    return {
        "x_prompt": nrm(ks[0], (BATCH, SEQ, D_MODEL), 1.0),
        "x_sample": nrm(ks[1], (DEC_BATCH, DEC_SEQ, D_MODEL), 1.0),
        "cache_k": nrm(ks[2], (DEPTH, n_pool, PAGE_SIZE, ATT_HEADS, HEAD_DIM), 1.0),
        "cache_v": nrm(ks[3], (DEPTH, n_pool, PAGE_SIZE, ATT_HEADS, HEAD_DIM), 1.0),
        "page_table": page_table,
        "ln1_g": gain(ks[5], (DEPTH, D_MODEL)),
        "w_in": nrm(ks[6], (DEPTH, D_MODEL, IN_WIDTH), D_MODEL ** -0.5),
        "q_norm_g": gain(ks[7], (DEPTH, ATT_HEADS, HEAD_DIM)),
        "k_norm_g": gain(ks[8], (DEPTH, ATT_HEADS, HEAD_DIM)),
        "gm_ln_g": gain(ks[9], (DEPTH, GM_HEADS, GM_HEAD_DIM)),
        "gm_ln_b": nrm(ks[10], (DEPTH, GM_HEADS, GM_HEAD_DIM), 0.02),
        "w_s": nrm(ks[11], (DEPTH, GM_HEADS, CHUNK, CHUNK), CHUNK ** -0.5),
        "b_s": gain(ks[12], (DEPTH, GM_HEADS, CHUNK)),
        "att_out_g": gain(ks[13], (DEPTH, ATT_WIDTH)),
        "gm_out_g": gain(ks[14], (DEPTH, GM_WIDTH)),
        "w_out": nrm(ks[15], (DEPTH, MIX_WIDTH, D_MODEL), MIX_WIDTH ** -0.5),
        "ln2_g": gain(ks[16], (DEPTH, D_MODEL)),
        "w_up": nrm(ks[17], (DEPTH, D_MODEL, D_FF), D_MODEL ** -0.5),
        "w_down": nrm(ks[18], (DEPTH, D_FF, D_MODEL), D_FF ** -0.5),
    }


def reference(x_prompt, x_sample, cache_k, cache_v, page_table, ln1_g, w_in, q_norm_g, k_norm_g,
              gm_ln_g, gm_ln_b, w_s, b_s, att_out_g, gm_out_g, w_out, ln2_g, w_up, w_down):
    slopes = _alibi_slopes()
    xp, xs = x_prompt, x_sample
    k_pr, v_pr, k_sm, v_sm, g_sm = [], [], [], [], []
    for l in range(DEPTH):
        pw = (ln1_g[l], w_in[l], q_norm_g[l], k_norm_g[l], gm_ln_g[l], gm_ln_b[l])
        ow = (att_out_g[l], gm_out_g[l], w_out[l], ln2_g[l], w_up[l], w_down[l])
        q, k, v, u, g = _project(xp, *pw)
        a = _moba_prompt(q, k, v, slopes)
        m = _sgu_prompt(u, g, w_s[l], b_s[l])
        xp = _merge_and_mlp(xp, a, m, *ow)
        k_pr.append(k)
        v_pr.append(v)
        q, k, v, u, g = _project(xs, *pw)
        a = _moba_sample(q, k, v, cache_k, cache_v, l, page_table, slopes)
        m = _sgu_sample(u, g, w_s[l], b_s[l])
        xs = _merge_and_mlp(xs, a, m, *ow)
        k_sm.append(k)
        v_sm.append(v)
        g_sm.append(g)
    return (xp, xs, jnp.stack(k_pr), jnp.stack(v_pr), jnp.stack(k_sm), jnp.stack(v_sm), jnp.stack(g_sm))
```

```python
import functools

import jax
import jax.numpy as jnp
from jax import lax
from jax.experimental import pallas as pl
from jax.experimental.pallas import tpu as pltpu

HEAD_DIM = 64
GM_HEAD_DIM = 64
MOBA_BLOCK = 256
MOBA_TOPK = 3
CHUNK = 128
PAGE_SIZE = 128
NORM_EPS = 1e-6

_F32 = jnp.float32
_BF16 = jnp.bfloat16
_NEG = -1e30
_VMEM_LIMIT = 56 * 1024 * 1024
_ROW_TILE = 512

_NT = (((1,), (1,)), ((), ()))
_TN = (((0,), (0,)), ((), ()))


def _dot(a, b):
    return jnp.dot(a, b, preferred_element_type=_F32)


def _split_bf16(x):
    hi = x.astype(_BF16)
    lo = (x - hi.astype(_F32)).astype(_BF16)
    return hi, lo


def _group_mean(x, gsum, width):
    hi, lo = _split_bf16(x)
    return (_dot(hi, gsum) + _dot(lo, gsum)) * (1.0 / width)


def _rms_rows(x, g):
    return x * lax.rsqrt(jnp.mean(x * x, axis=-1, keepdims=True) + NORM_EPS) * g


def _proj_kernel(x_ref, ln_g_ref, w_ref, qg_ref, kg_ref, glg_ref, glb_ref, gsum_ref,
                 wmix_ref, bmix_ref, gmo_ref, *refs, att_w, gm_w, prompt):
    if prompt:
        k_ref, v_ref, qt_ref, kh_ref, vt_ref, ksum_ref, mixn_ref = refs
    else:
        k_ref, v_ref, q_ref, g_ref, mixn_ref = refs
    tm = x_ref.shape[0]
    gsum = gsum_ref[...]

    xn = _rms_rows(x_ref[...], ln_g_ref[...]).astype(_BF16)

    def zcols(lo, width):
        return _dot(xn, w_ref[:, lo:lo + width])

    def head_rms(z, g):
        return z * lax.rsqrt(_group_mean(z * z, gsum, HEAD_DIM) + NORM_EPS) * g

    q = head_rms(zcols(0, att_w), qg_ref[...])
    k = head_rms(zcols(att_w, att_w), kg_ref[...])
    v = zcols(2 * att_w, att_w)
    k_ref[...] = k
    v_ref[...] = v
    if prompt:
        n_pair = att_w // 128
        qt_ref[0] = q.T.reshape(n_pair, 128, tm)
        kb = k.astype(_BF16)
        vtb = v.T.astype(_BF16)
        for j in range(tm // MOBA_BLOCK):
            blk = slice(j * MOBA_BLOCK, (j + 1) * MOBA_BLOCK)
            for p in range(n_pair):
                pair = slice(p * 128, (p + 1) * 128)
                kh_ref[0, p, j] = kb[blk, pair]
                vt_ref[0, p, j] = vtb[pair, blk]
            ksum_ref[0, j:j + 1, :] = jnp.sum(k[blk], axis=0, keepdims=True)
    else:
        q_ref[...] = q

    u = jax.nn.gelu(zcols(3 * att_w, gm_w))
    gg = jax.nn.gelu(zcols(3 * att_w + gm_w, gm_w))
    mu = _group_mean(gg, gsum, GM_HEAD_DIM)
    cen = gg - mu
    var = _group_mean(cen * cen, gsum, GM_HEAD_DIM)
    g = cen * lax.rsqrt(var + NORM_EPS) * glg_ref[...] + glb_ref[...]
    if not prompt:
        g_ref[...] = g

    n_grp = gm_w // GM_HEAD_DIM
    gb = g.astype(_BF16)
    wmix = wmix_ref[...]
    bmix = bmix_ref[...]
    lane_grp = lax.broadcasted_iota(jnp.int32, (CHUNK, gm_w), 1) // GM_HEAD_DIM
    pieces = []
    for c in range(tm // CHUNK):
        rows = slice(c * CHUNK, (c + 1) * CHUNK)
        stack = jnp.concatenate(
            [jnp.where(lane_grp == h, gb[rows], jnp.zeros_like(gb[rows])) for h in range(n_grp)],
            axis=0)
        mixed = _dot(wmix, stack) + bmix
        pieces.append(u[rows] * mixed)
    m = jnp.concatenate(pieces, axis=0)
    mixn_ref[...] = _rms_rows(m, gmo_ref[...]).astype(_BF16)


def _project(x2d, seq, wts, wmix, bmix, prompt, tm):
    n, d_model = x2d.shape
    att_w = wts["qg"].shape[1]
    gm_w = wts["glg"].shape[1]
    steps = n // tm
    per_seq = seq // tm if prompt else 1
    row = lambda i: (i, 0)
    full2 = lambda i: (0, 0)
    in_specs = [
        pl.BlockSpec((tm, d_model), row),
        pl.BlockSpec((1, d_model), full2),
        pl.BlockSpec(wts["w_in"].shape, full2),
        pl.BlockSpec((1, att_w), full2),
        pl.BlockSpec((1, att_w), full2),
        pl.BlockSpec((1, gm_w), full2),
        pl.BlockSpec((1, gm_w), full2),
        pl.BlockSpec(wts["gsum"].shape, full2),
        pl.BlockSpec(wmix.shape, full2),
        pl.BlockSpec(bmix.shape, full2),
        pl.BlockSpec((1, gm_w), full2),
    ]
    f32_rows = jax.ShapeDtypeStruct((n, att_w), _F32)
    if prompt:
        batch = n // seq
        n_pair = att_w // 128
        blk_per_tile = tm // MOBA_BLOCK
        nb = seq // MOBA_BLOCK
        blocks = lambda i: (i // per_seq, 0, i % per_seq, 0, 0)
        out_shape = (
            f32_rows, f32_rows,
            jax.ShapeDtypeStruct((batch, n_pair, 128, seq), _F32),
            jax.ShapeDtypeStruct((batch, n_pair, nb, MOBA_BLOCK, 128), _BF16),
            jax.ShapeDtypeStruct((batch, n_pair, nb, 128, MOBA_BLOCK), _BF16),
            jax.ShapeDtypeStruct((steps, blk_per_tile, att_w), _F32),
            jax.ShapeDtypeStruct((n, gm_w), _BF16),
        )
        out_specs = (
            pl.BlockSpec((tm, att_w), row), pl.BlockSpec((tm, att_w), row),
            pl.BlockSpec((1, n_pair, 128, tm), lambda i: (i // per_seq, 0, 0, i % per_seq)),
            pl.BlockSpec((1, n_pair, blk_per_tile, MOBA_BLOCK, 128), blocks),
            pl.BlockSpec((1, n_pair, blk_per_tile, 128, MOBA_BLOCK), blocks),
            pl.BlockSpec((1, blk_per_tile, att_w), lambda i: (i, 0, 0)),
            pl.BlockSpec((tm, gm_w), row),
        )
    else:
        out_shape = (f32_rows, f32_rows, f32_rows,
                     jax.ShapeDtypeStruct((n, gm_w), _F32),
                     jax.ShapeDtypeStruct((n, gm_w), _BF16))
        out_specs = tuple(pl.BlockSpec((tm, att_w), row) for _ in range(5))
    return pl.pallas_call(
        functools.partial(_proj_kernel, att_w=att_w, gm_w=gm_w, prompt=prompt),
        grid=(steps,),
        in_specs=in_specs,
        out_specs=out_specs,
        out_shape=out_shape,
        compiler_params=pltpu.CompilerParams(dimension_semantics=("arbitrary",),
                                             vmem_limit_bytes=_VMEM_LIMIT),
        name="project_prompt" if prompt else "project_sample",
    )(x2d, wts["ln1_g"], wts["w_in"], wts["qg"], wts["kg"], wts["glg"], wts["glb"], wts["gsum"],
      wmix, bmix, wts["gmo"])


def _mlp_kernel(x_ref, a_ref, mixn_ref, ag_ref, wo_a_ref, wo_m_ref, ln2_ref, wup_ref, wdn_ref,
                o_ref, *, a_transposed, ff_chunk):
    a = a_ref[0].T if a_transposed else a_ref[...]
    an = _rms_rows(a, ag_ref[...]).astype(_BF16)
    x1 = x_ref[...] + _dot(an, wo_a_ref[...]) + _dot(mixn_ref[...], wo_m_ref[...])
    hn = _rms_rows(x1, ln2_ref[...]).astype(_BF16)
    y = None
    for c in range(wup_ref.shape[1] // ff_chunk):
        cols = slice(c * ff_chunk, (c + 1) * ff_chunk)
        h = jnp.square(jnp.maximum(_dot(hn, wup_ref[:, cols]), 0.0)).astype(_BF16)
        d = _dot(h, wdn_ref[cols, :])
        y = d if y is None else y + d
    o_ref[...] = x1 + y


def _merge_mlp(x2d, a, mixn, wts, seq, tm, a_transposed):
    n, d_model = x2d.shape
    att_w = wts["wo_a"].shape[0]
    gm_w = wts["wo_m"].shape[0]
    d_ff = wts["w_up"].shape[1]
    row = lambda i: (i, 0)
    full2 = lambda i: (0, 0)
    once = dict(pipeline_mode=pl.Buffered(1))
    if a_transposed:
        per_seq = seq // tm
        a_spec = pl.BlockSpec((1, att_w, tm), lambda i: (i // per_seq, 0, i % per_seq))
    else:
        a_spec = pl.BlockSpec((tm, att_w), row)
    return pl.pallas_call(
        functools.partial(_mlp_kernel, a_transposed=a_transposed, ff_chunk=min(d_ff, 1024)),
        grid=(n // tm,),
        in_specs=[
            pl.BlockSpec((tm, d_model), row),
            a_spec,
            pl.BlockSpec((tm, gm_w), row),
            pl.BlockSpec((1, att_w), full2),
            pl.BlockSpec((att_w, d_model), full2, **once),
            pl.BlockSpec((gm_w, d_model), full2, **once),
            pl.BlockSpec((1, d_model), full2),
            pl.BlockSpec((d_model, d_ff), full2, **once),
            pl.BlockSpec((d_ff, d_model), full2, **once),
        ],
        out_specs=pl.BlockSpec((tm, d_model), row),
        out_shape=jax.ShapeDtypeStruct((n, d_model), _F32),
        compiler_params=pltpu.CompilerParams(dimension_semantics=("arbitrary",),
                                             vmem_limit_bytes=_VMEM_LIMIT),
        name="merge_mlp_prompt" if a_transposed else "merge_mlp_sample",
    )(x2d, a, mixn, wts["ag"], wts["wo_a"], wts["wo_m"], wts["ln2_g"], wts["w_up"], wts["w_down"])


def _topk_rows(gate, n_valid, k):
    nb = gate.shape[0]
    rows = lax.broadcasted_iota(jnp.int32, gate.shape, 0)
    valid = rows < n_valid
    g = jnp.where(valid, gate, -jnp.inf)
    sel = jnp.zeros(gate.shape, jnp.bool_)
    for _ in range(min(k, nb)):
        best = jnp.max(g, axis=0, keepdims=True)
        first = jnp.min(jnp.where(g == best, rows, nb), axis=0, keepdims=True)
        hit = rows == first
        sel = jnp.logical_or(sel, hit)
        g = jnp.where(hit, -jnp.inf, g)
    return jnp.logical_and(sel, valid)


def _moba_prompt_kernel(slopes_ref, qt_ref, kh_ref, vt_ref, ksum_ref, bias_ref, o_ref, sel_ref):
    pair = pl.program_id(1)
    i = pl.program_id(2)
    tq = qt_ref.shape[3]
    scale = HEAD_DIM ** -0.5
    qt = qt_ref[0, 0]
    srow = lax.broadcasted_iota(jnp.int32, qt.shape, 0)
    krow = lax.broadcasted_iota(jnp.int32, (MOBA_BLOCK, tq), 0)
    qcol = lax.broadcasted_iota(jnp.int32, (MOBA_BLOCK, tq), 1)
    causal = krow <= qcol
    kmean = ksum_ref[0, 0] * (1.0 / MOBA_BLOCK)

    for r in range(2):
        rows = slice(r * HEAD_DIM, (r + 1) * HEAD_DIM)
        in_head = jnp.logical_and(srow >= r * HEAD_DIM, srow < (r + 1) * HEAD_DIM)
        q_f32 = jnp.where(in_head, qt, 0.0)
        q_bf = (q_f32 * scale).astype(_BF16)
        slope = slopes_ref[2 * pair + r]
        bias = bias_ref[r]

        gate = jnp.dot(kmean, q_f32, precision=lax.Precision.HIGHEST,
                       preferred_element_type=_F32)
        sel_ref[...] = _topk_rows(gate, i, MOBA_TOPK).astype(_F32)

        t = _dot(kh_ref[0, 0, i], q_bf) + bias
        t = jnp.where(causal, t, -jnp.inf)
        m = jnp.max(t, axis=0, keepdims=True)
        p = jnp.exp(t - m)
        l = jnp.sum(p, axis=0, keepdims=True)
        acc = _dot(vt_ref[0, 0, i, rows, :], p.astype(_BF16))

        def past_block(n, carry):
            m, l, acc = carry
            t = _dot(kh_ref[0, 0, n], q_bf) + bias
            off = slope * ((n - i) * MOBA_BLOCK).astype(_F32)
            chosen = sel_ref[pl.ds(n, 1), :] > 0.5
            blk_max = jnp.max(t, axis=0, keepdims=True) + off
            m_new = jnp.maximum(m, jnp.where(chosen, blk_max, _NEG))
            p = jnp.exp(t - jnp.where(chosen, m_new - off, jnp.inf))
            alpha = jnp.exp(m - m_new)
            l = alpha * l + jnp.sum(p, axis=0, keepdims=True)
            acc = alpha * acc + _dot(vt_ref[0, 0, n, rows, :], p.astype(_BF16))
            return m_new, l, acc

        m, l, acc = lax.fori_loop(0, i, past_block, (m, l, acc))
        o_ref[0, 0, rows, :] = acc / l


def _moba_prompt(qt, kh, vt, ksum, slopes, bias):
    batch, n_pair, _, seq = qt.shape
    nb = kh.shape[2]
    tq = MOBA_BLOCK
    return pl.pallas_call(
        _moba_prompt_kernel,
        grid=(batch, n_pair, nb),
        in_specs=[
            pl.BlockSpec(memory_space=pltpu.SMEM),
            pl.BlockSpec((1, 1, 128, tq), lambda b, p, i: (b, p, 0, i)),
            pl.BlockSpec((1, 1, nb, MOBA_BLOCK, 128), lambda b, p, i: (b, p, 0, 0, 0)),
            pl.BlockSpec((1, 1, nb, 128, MOBA_BLOCK), lambda b, p, i: (b, p, 0, 0, 0)),
            pl.BlockSpec((1, 1, nb, 128), lambda b, p, i: (b, p, 0, 0)),
            pl.BlockSpec((2, MOBA_BLOCK, tq), lambda b, p, i: (p, 0, 0)),
        ],
        out_specs=pl.BlockSpec((1, 1, 128, tq), lambda b, p, i: (b, p, 0, i)),
        out_shape=jax.ShapeDtypeStruct((batch, n_pair, 128, seq), _F32),
        scratch_shapes=[pltpu.VMEM((nb, tq), _F32)],
        compiler_params=pltpu.CompilerParams(
            dimension_semantics=("arbitrary", "arbitrary", "arbitrary"),
            vmem_limit_bytes=_VMEM_LIMIT),
        name="moba_prompt",
    )(slopes, qt, kh, vt, ksum, bias)


_RING = 32
_AHEAD = 24
_QCOLS = 128


def _moba_sample_kernel(pt_ref, q_ref, kn_ref, vn_ref, cols_ref, ck_ref, cv_ref, o_ref,
                        ring_ref, sem_ref, ksum_ref, st_ref, p_ref, *, n_heads):
    b = pl.program_id(0)
    n_seq = pl.num_programs(0)
    n_pages = pt_ref.shape[1]
    tn, att_w = q_ref.shape
    n_keys = n_pages * PAGE_SIZE
    ppb = MOBA_BLOCK // PAGE_SIZE
    nb = n_pages // ppb
    steps = 2 * n_pages
    total = n_seq * steps
    g0 = b * steps

    def copy(src_ref, page, slot):
        return pltpu.make_async_copy(src_ref.at[page], ring_ref.at[slot], sem_ref.at[slot])

    def fetch(g):
        @pl.when(g < total)
        def _():
            seq = g // steps
            s = g % steps
            slot = g % _RING

            @pl.when(s < n_pages)
            def _():
                copy(ck_ref, pt_ref[seq, s], slot).start()

            @pl.when(s >= n_pages)
            def _():
                copy(cv_ref, pt_ref[seq, s - n_pages], slot).start()

    def wait(g):
        slot = g % _RING
        copy(ck_ref, 0, slot).wait()
        return slot

    @pl.when(b == 0)
    def _():
        for g in range(_AHEAD):
            fetch(jnp.int32(g))

    q = q_ref[...]
    qrow = lax.broadcasted_iota(jnp.int32, (_QCOLS, att_w), 0)
    qlane = lax.broadcasted_iota(jnp.int32, (_QCOLS, att_w), 1)
    q_rep = jnp.concatenate([q] * (_QCOLS // tn), axis=0)
    qbd_f32 = jnp.where(qrow // tn == qlane // HEAD_DIM, q_rep, 0.0)
    qbd = (qbd_f32 * HEAD_DIM ** -0.5).astype(_BF16)

    def k_block(n, carry):
        ksum = jnp.zeros((1, att_w), _F32)
        for r in range(ppb):
            j = n * ppb + r
            g = g0 + j
            slot = wait(g)
            kp = ring_ref[slot]
            ksum = ksum + jnp.sum(kp, axis=0, keepdims=True)
            st_ref[pl.ds(pl.multiple_of(j * PAGE_SIZE, PAGE_SIZE), PAGE_SIZE), :] = lax.dot_general(
                kp.astype(_BF16), qbd, _NT, preferred_element_type=_F32)
            fetch(g + _AHEAD)
        ksum_ref[pl.ds(n, 1), :] = ksum
        return carry

    lax.fori_loop(0, nb, k_block, 0)

    gate = lax.dot_general(ksum_ref[...] * (1.0 / MOBA_BLOCK), qbd_f32, _NT,
                           precision=lax.Precision.HIGHEST, preferred_element_type=_F32)
    sel = _topk_rows(gate, nb, MOBA_TOPK)

    slope = cols_ref[0:1, :]
    tq = cols_ref[1:2, :]
    past = float(n_keys)
    keypos = lax.broadcasted_iota(jnp.int32, (n_keys, _QCOLS), 0).astype(_F32)
    logits = st_ref[...] - slope * ((past + tq) - keypos)
    logits = jnp.where(sel[:, None, :], logits.reshape(nb, MOBA_BLOCK, _QCOLS), -jnp.inf)
    tk = lax.broadcasted_iota(jnp.int32, (tn, _QCOLS), 0).astype(_F32)
    dist_new = tq - tk
    s_new = lax.dot_general(kn_ref[...].astype(_BF16), qbd, _NT, preferred_element_type=_F32)
    l_new = jnp.where(dist_new >= 0.0, s_new - slope * dist_new, -jnp.inf)
    m = jnp.maximum(jnp.max(jnp.max(logits, axis=1), axis=0, keepdims=True),
                    jnp.max(l_new, axis=0, keepdims=True))
    p = jnp.exp(logits - m)
    p_new = jnp.exp(l_new - m)
    inv = 1.0 / (jnp.sum(jnp.sum(p, axis=1), axis=0, keepdims=True)
                 + jnp.sum(p_new, axis=0, keepdims=True))
    p_ref[...] = (p * inv).reshape(n_keys, _QCOLS)
    p_new = (p_new * inv).astype(_BF16)

    def v_page(j, acc):
        g = g0 + n_pages + j
        slot = wait(g)
        vp = ring_ref[slot].astype(_BF16)
        pj = p_ref[pl.ds(pl.multiple_of(j * PAGE_SIZE, PAGE_SIZE), PAGE_SIZE), :]
        acc = acc + _dot(pj.T.astype(_BF16), vp)
        fetch(g + _AHEAD)
        return acc

    acc = lax.dot_general(p_new, vn_ref[...].astype(_BF16), _TN, preferred_element_type=_F32)
    acc = lax.fori_loop(0, n_pages, v_page, acc)

    lane_head = lax.broadcasted_iota(jnp.int32, (tn, att_w), 1) // HEAD_DIM
    out = jnp.zeros((tn, att_w), _F32)
    for h in range(n_heads):
        out = out + jnp.where(lane_head == h, acc[h * tn:(h + 1) * tn, :], 0.0)
    o_ref[...] = out


def _moba_sample(q, k_new, v_new, cache_k, cache_v, pages, cols, tn):
    n, att_w = q.shape
    n_seq, n_pages = pages.shape
    n_keys = n_pages * PAGE_SIZE
    row = lambda b, pt: (b, 0)
    return pl.pallas_call(
        functools.partial(_moba_sample_kernel, n_heads=att_w // HEAD_DIM),
        grid_spec=pltpu.PrefetchScalarGridSpec(
            num_scalar_prefetch=1,
            grid=(n_seq,),
            in_specs=[
                pl.BlockSpec((tn, att_w), row),
                pl.BlockSpec((tn, att_w), row),
                pl.BlockSpec((tn, att_w), row),
                pl.BlockSpec(cols.shape, lambda b, pt: (0, 0)),
                pl.BlockSpec(memory_space=pl.ANY),
                pl.BlockSpec(memory_space=pl.ANY),
            ],
            out_specs=pl.BlockSpec((tn, att_w), row),
            scratch_shapes=[
                pltpu.VMEM((_RING, PAGE_SIZE, att_w), _F32),
                pltpu.SemaphoreType.DMA((_RING,)),
                pltpu.VMEM((n_keys // MOBA_BLOCK, att_w), _F32),
                pltpu.VMEM((n_keys, _QCOLS), _F32),
                pltpu.VMEM((n_keys, _QCOLS), _F32),
            ],
        ),
        out_shape=jax.ShapeDtypeStruct((n, att_w), _F32),
        compiler_params=pltpu.CompilerParams(dimension_semantics=("arbitrary",),
                                             vmem_limit_bytes=_VMEM_LIMIT),
        name="moba_sample",
    )(pages, q, k_new, v_new, cols, cache_k, cache_v)


def _alibi_slopes(n_heads):
    h = jnp.arange(1, n_heads + 1, dtype=_F32)
    return jnp.exp2(-8.0 * h / n_heads)


def kernel(x_prompt, x_sample, cache_k, cache_v, page_table, ln1_g, w_in, q_norm_g, k_norm_g,
           gm_ln_g, gm_ln_b, w_s, b_s, att_out_g, gm_out_g, w_out, ln2_g, w_up, w_down):
    batch, seq, d_model = x_prompt.shape
    dec_batch, tn, _ = x_sample.shape
    depth, n_pool, page_size, n_heads, head_dim = cache_k.shape
    gm_heads, gm_dim = gm_ln_g.shape[1:]
    att_w = n_heads * head_dim
    gm_w = gm_heads * gm_dim
    n_pages = page_table.shape[1]
    assert (head_dim, gm_dim, page_size, w_s.shape[-1]) == (HEAD_DIM, GM_HEAD_DIM, PAGE_SIZE, CHUNK)
    assert att_w == gm_w and att_w % 128 == 0 and n_heads * tn <= _QCOLS and _QCOLS % tn == 0
    assert seq % MOBA_BLOCK == 0 and CHUNK % tn == 0
    assert (n_pages * PAGE_SIZE) % MOBA_BLOCK == 0 and n_pages * PAGE_SIZE >= MOBA_BLOCK
    tm = min(_ROW_TILE, seq)
    tm_s = min(_ROW_TILE, dec_batch * tn)
    assert seq % tm == 0 and tm % MOBA_BLOCK == 0 and (dec_batch * tn) % tm_s == 0 and tm_s % CHUNK == 0
    n_pair = att_w // 128
    nb = seq // MOBA_BLOCK

    slopes = _alibi_slopes(n_heads)
    rel = jnp.arange(MOBA_BLOCK, dtype=_F32)
    bias = slopes[:, None, None] * (rel[:, None] - rel[None, :])
    group = jnp.arange(att_w) // HEAD_DIM
    gsum = (group[:, None] == group[None, :]).astype(_BF16)
    col = jnp.arange(_QCOLS)
    live = col < n_heads * tn
    cols = jnp.stack([jnp.where(live, slopes[jnp.minimum(col // tn, n_heads - 1)], 0.0),
                      jnp.where(live, col % tn, 0).astype(_F32)])

    xp = x_prompt.reshape(batch * seq, d_model)
    xs = x_sample.reshape(dec_batch * tn, d_model)
    ck = cache_k.reshape(depth * n_pool, page_size, att_w)
    cv = cache_v.reshape(depth * n_pool, page_size, att_w)
    outs = [[] for _ in range(5)]
    for l in range(depth):
        wts = dict(
            ln1_g=ln1_g[l][None], w_in=w_in[l].astype(_BF16),
            qg=q_norm_g[l].reshape(1, att_w), kg=k_norm_g[l].reshape(1, att_w),
            glg=gm_ln_g[l].reshape(1, gm_w), glb=gm_ln_b[l].reshape(1, gm_w),
            gsum=gsum, gmo=gm_out_g[l][None], ag=att_out_g[l][None],
            wo_a=w_out[l][:att_w].astype(_BF16), wo_m=w_out[l][att_w:].astype(_BF16),
            ln2_g=ln2_g[l][None], w_up=w_up[l].astype(_BF16), w_down=w_down[l].astype(_BF16))
        w_tril = jnp.tril(w_s[l])
        wmix_p = w_tril.transpose(1, 0, 2).reshape(CHUNK, gm_heads * CHUNK).astype(_BF16)
        bmix_p = jnp.repeat(b_s[l].T, GM_HEAD_DIM, axis=1)
        eye = jnp.eye(CHUNK // tn, dtype=_F32)
        w_blk = jnp.einsum("ab,hij->haibj", eye, w_tril[:, :tn, :tn]).reshape(gm_heads, CHUNK, CHUNK)
        wmix_s = w_blk.transpose(1, 0, 2).reshape(CHUNK, gm_heads * CHUNK).astype(_BF16)
        bmix_s = jnp.tile(jnp.repeat(b_s[l][:, :tn].T, GM_HEAD_DIM, axis=1), (CHUNK // tn, 1))

        k_p, v_p, qt, kh, vt, ksum, mixn_p = _project(xp, seq, wts, wmix_p, bmix_p, True, tm)
        ksum = ksum.reshape(batch, nb, n_pair, 128).transpose(0, 2, 1, 3)
        a_t = _moba_prompt(qt, kh, vt, ksum, slopes, bias)
        xp = _merge_mlp(xp, a_t.reshape(batch, att_w, seq), mixn_p, wts, seq, tm, True)

        k_s, v_s, q_s, g_s, mixn_s = _project(xs, tn, wts, wmix_s, bmix_s, False, tm_s)
        a_s = _moba_sample(q_s, k_s, v_s, ck, cv, page_table + l * n_pool, cols, tn)
        xs = _merge_mlp(xs, a_s, mixn_s, wts, tn, tm_s, False)

        outs[0].append(k_p.reshape(batch, seq, n_heads, head_dim))
        outs[1].append(v_p.reshape(batch, seq, n_heads, head_dim))
        outs[2].append(k_s.reshape(dec_batch, tn, n_heads, head_dim))
        outs[3].append(v_s.reshape(dec_batch, tn, n_heads, head_dim))
        outs[4].append(g_s.reshape(dec_batch, tn, gm_heads, gm_dim))
    return (xp.reshape(batch, seq, d_model), xs.reshape(dec_batch, tn, d_model),
            *(jnp.stack(o) for o in outs))
```

```python
import functools

import jax
import jax.numpy as jnp
from jax import lax
from jax.experimental import pallas as pl
from jax.experimental.pallas import tpu as pltpu

HEAD_DIM = 64
GM_HEAD_DIM = 64
MOBA_BLOCK = 256
MOBA_TOPK = 3
CHUNK = 128
PAGE_SIZE = 128
NORM_EPS = 1e-6

_F32 = jnp.float32
_BF16 = jnp.bfloat16
_NEG = -1e30
_VMEM_LIMIT = 56 * 1024 * 1024
_ROW_TILE = 512
_LOG2E = 1.4426950408889634
_V_ROWS = 80
_POS_TERMS = 3
_KV_GROUP = 2

_NT = (((1,), (1,)), ((), ()))
_TN = (((0,), (0,)), ((), ()))


def _dot(a, b):
    return jnp.dot(a, b, preferred_element_type=_F32)


def _split_bf16(x):
    hi = x.astype(_BF16)
    lo = (x - hi.astype(_F32)).astype(_BF16)
    return hi, lo


def _group_mean(x, gsum, width):
    hi, lo = _split_bf16(x)
    return (_dot(hi, gsum) + _dot(lo, gsum)) * (1.0 / width)


def _rms_rows(x, g):
    return x * lax.rsqrt(jnp.mean(x * x, axis=-1, keepdims=True) + NORM_EPS) * g


def _proj_kernel(x_ref, ln_g_ref, w_ref, qg_ref, kg_ref, glg_ref, glb_ref, gsum_ref,
                 wmix_ref, bmix_ref, gmo_ref, *refs, att_w, gm_w, prompt):
    if prompt:
        (kpos_ref, qpos_ref,
         k_ref, v_ref, qt_ref, qa_ref, ka_ref, va_ref, ksum_ref, mixn_ref) = refs
    else:
        k_ref, v_ref, q_ref, g_ref, mixn_ref = refs
    tm = x_ref.shape[0]
    gsum = gsum_ref[...]

    xn = _rms_rows(x_ref[...], ln_g_ref[...]).astype(_BF16)

    def zcols(lo, width):
        return _dot(xn, w_ref[:, lo:lo + width])

    def head_rms(z, g):
        return z * lax.rsqrt(_group_mean(z * z, gsum, HEAD_DIM) + NORM_EPS) * g

    q = head_rms(zcols(0, att_w), qg_ref[...])
    k = head_rms(zcols(att_w, att_w), kg_ref[...])
    v = zcols(2 * att_w, att_w)
    k_ref[...] = k
    v_ref[...] = v
    if prompt:
        n_pair = att_w // 128
        qt = q.T
        qt_ref[0] = qt.reshape(n_pair, 128, tm)
        qtb = (qt * (HEAD_DIM ** -0.5 * _LOG2E)).astype(_BF16)
        kb = k.astype(_BF16)
        vtb = v.T.astype(_BF16)
        lane_half = lax.broadcasted_iota(jnp.int32, (MOBA_BLOCK, 128), 1) // HEAD_DIM
        row_half = lax.broadcasted_iota(jnp.int32, (128, tm), 0) // HEAD_DIM
        ones_row = (lax.broadcasted_iota(jnp.int32, (_V_ROWS - HEAD_DIM, MOBA_BLOCK), 0) == 0
                    ).astype(_BF16)
        for h in range(att_w // HEAD_DIM):
            pair = slice((h // 2) * 128, (h // 2 + 1) * 128)
            qa_ref[0, h] = jnp.where(row_half == h % 2, qtb[pair], qpos_ref[h])
            for j in range(tm // MOBA_BLOCK):
                blk = slice(j * MOBA_BLOCK, (j + 1) * MOBA_BLOCK)
                ka_ref[0, h, j] = jnp.where(lane_half == h % 2, kb[blk, pair], kpos_ref[h, j])
                va_ref[0, h, j, 0:HEAD_DIM, :] = vtb[h * HEAD_DIM:(h + 1) * HEAD_DIM, blk]
                va_ref[0, h, j, HEAD_DIM:_V_ROWS, :] = ones_row
        for j in range(tm // MOBA_BLOCK):
            blk = slice(j * MOBA_BLOCK, (j + 1) * MOBA_BLOCK)
            ksum_ref[0, j:j + 1, :] = jnp.sum(k[blk], axis=0, keepdims=True)
    else:
        q_ref[...] = q

    u = jax.nn.gelu(zcols(3 * att_w, gm_w))
    gg = jax.nn.gelu(zcols(3 * att_w + gm_w, gm_w))
    mu = _group_mean(gg, gsum, GM_HEAD_DIM)
    cen = gg - mu
    var = _group_mean(cen * cen, gsum, GM_HEAD_DIM)
    g = cen * lax.rsqrt(var + NORM_EPS) * glg_ref[...] + glb_ref[...]
    if not prompt:
        g_ref[...] = g

    n_grp = gm_w // GM_HEAD_DIM
    gb = g.astype(_BF16)
    wmix = wmix_ref[...]
    bmix = bmix_ref[...]
    lane_grp = lax.broadcasted_iota(jnp.int32, (CHUNK, gm_w), 1) // GM_HEAD_DIM
    pieces = []
    for c in range(tm // CHUNK):
        rows = slice(c * CHUNK, (c + 1) * CHUNK)
        stack = jnp.concatenate(
            [jnp.where(lane_grp == h, gb[rows], jnp.zeros_like(gb[rows])) for h in range(n_grp)],
            axis=0)
        mixed = _dot(wmix, stack) + bmix
        pieces.append(u[rows] * mixed)
    m = jnp.concatenate(pieces, axis=0)
    mixn_ref[...] = _rms_rows(m, gmo_ref[...]).astype(_BF16)


def _project(x2d, seq, wts, wmix, bmix, prompt, tm, tables=()):
    n, d_model = x2d.shape
    att_w = wts["qg"].shape[1]
    gm_w = wts["glg"].shape[1]
    steps = n // tm
    per_seq = seq // tm if prompt else 1
    row = lambda i: (i, 0)
    full2 = lambda i: (0, 0)
    in_specs = [
        pl.BlockSpec((tm, d_model), row),
        pl.BlockSpec((1, d_model), full2),
        pl.BlockSpec(wts["w_in"].shape, full2),
        pl.BlockSpec((1, att_w), full2),
        pl.BlockSpec((1, att_w), full2),
        pl.BlockSpec((1, gm_w), full2),
        pl.BlockSpec((1, gm_w), full2),
        pl.BlockSpec(wts["gsum"].shape, full2),
        pl.BlockSpec(wmix.shape, full2),
        pl.BlockSpec(bmix.shape, full2),
        pl.BlockSpec((1, gm_w), full2),
    ]
    f32_rows = jax.ShapeDtypeStruct((n, att_w), _F32)
    if prompt:
        batch = n // seq
        n_pair = att_w // 128
        n_heads = att_w // HEAD_DIM
        blk_per_tile = tm // MOBA_BLOCK
        nb = seq // MOBA_BLOCK
        in_specs += [
            pl.BlockSpec((n_heads, blk_per_tile, MOBA_BLOCK, 128), lambda i: (0, i % per_seq, 0, 0)),
            pl.BlockSpec((n_heads, 128, tm), lambda i: (0, 0, i % per_seq)),
        ]
        cols = lambda i: (i // per_seq, 0, 0, i % per_seq)
        blocks = lambda i: (i // per_seq, 0, i % per_seq, 0, 0)
        out_shape = (
            f32_rows, f32_rows,
            jax.ShapeDtypeStruct((batch, n_pair, 128, seq), _F32),
            jax.ShapeDtypeStruct((batch, n_heads, 128, seq), _BF16),
            jax.ShapeDtypeStruct((batch, n_heads, nb, MOBA_BLOCK, 128), _BF16),
            jax.ShapeDtypeStruct((batch, n_heads, nb, _V_ROWS, MOBA_BLOCK), _BF16),
            jax.ShapeDtypeStruct((steps, blk_per_tile, att_w), _F32),
            jax.ShapeDtypeStruct((n, gm_w), _BF16),
        )
        out_specs = (
            pl.BlockSpec((tm, att_w), row), pl.BlockSpec((tm, att_w), row),
            pl.BlockSpec((1, n_pair, 128, tm), cols),
            pl.BlockSpec((1, n_heads, 128, tm), cols),
            pl.BlockSpec((1, n_heads, blk_per_tile, MOBA_BLOCK, 128), blocks),
            pl.BlockSpec((1, n_heads, blk_per_tile, _V_ROWS, MOBA_BLOCK), blocks),
            pl.BlockSpec((1, blk_per_tile, att_w), lambda i: (i, 0, 0)),
            pl.BlockSpec((tm, gm_w), row),
        )
    else:
        out_shape = (f32_rows, f32_rows, f32_rows,
                     jax.ShapeDtypeStruct((n, gm_w), _F32),
                     jax.ShapeDtypeStruct((n, gm_w), _BF16))
        out_specs = tuple(pl.BlockSpec((tm, att_w), row) for _ in range(5))
    return pl.pallas_call(
        functools.partial(_proj_kernel, att_w=att_w, gm_w=gm_w, prompt=prompt),
        grid=(steps,),
        in_specs=in_specs,
        out_specs=out_specs,
        out_shape=out_shape,
        compiler_params=pltpu.CompilerParams(dimension_semantics=("arbitrary",),
                                             vmem_limit_bytes=_VMEM_LIMIT),
        name="project_prompt" if prompt else "project_sample",
    )(x2d, wts["ln1_g"], wts["w_in"], wts["qg"], wts["kg"], wts["glg"], wts["glb"], wts["gsum"],
      wmix, bmix, wts["gmo"], *tables)


def _mlp_kernel(x_ref, a_ref, mixn_ref, ag_ref, wo_a_ref, wo_m_ref, ln2_ref, wup_ref, wdn_ref,
                o_ref, *, a_transposed, ff_chunk):
    a = a_ref[0].T if a_transposed else a_ref[...]
    an = _rms_rows(a, ag_ref[...]).astype(_BF16)
    x1 = x_ref[...] + _dot(an, wo_a_ref[...]) + _dot(mixn_ref[...], wo_m_ref[...])
    hn = _rms_rows(x1, ln2_ref[...]).astype(_BF16)
    y = None
    for c in range(wup_ref.shape[1] // ff_chunk):
        cols = slice(c * ff_chunk, (c + 1) * ff_chunk)
        h = jnp.square(jnp.maximum(_dot(hn, wup_ref[:, cols]), 0.0)).astype(_BF16)
        d = _dot(h, wdn_ref[cols, :])
        y = d if y is None else y + d
    o_ref[...] = x1 + y


def _merge_mlp(x2d, a, mixn, wts, seq, tm, a_transposed):
    n, d_model = x2d.shape
    att_w = wts["wo_a"].shape[0]
    gm_w = wts["wo_m"].shape[0]
    d_ff = wts["w_up"].shape[1]
    row = lambda i: (i, 0)
    full2 = lambda i: (0, 0)
    once = dict(pipeline_mode=pl.Buffered(1))
    if a_transposed:
        per_seq = seq // tm
        a_spec = pl.BlockSpec((1, att_w, tm), lambda i: (i // per_seq, 0, i % per_seq))
    else:
        a_spec = pl.BlockSpec((tm, att_w), row)
    return pl.pallas_call(
        functools.partial(_mlp_kernel, a_transposed=a_transposed, ff_chunk=min(d_ff, 1024)),
        grid=(n // tm,),
        in_specs=[
            pl.BlockSpec((tm, d_model), row),
            a_spec,
            pl.BlockSpec((tm, gm_w), row),
            pl.BlockSpec((1, att_w), full2),
            pl.BlockSpec((att_w, d_model), full2, **once),
            pl.BlockSpec((gm_w, d_model), full2, **once),
            pl.BlockSpec((1, d_model), full2),
            pl.BlockSpec((d_model, d_ff), full2, **once),
            pl.BlockSpec((d_ff, d_model), full2, **once),
        ],
        out_specs=pl.BlockSpec((tm, d_model), row),
        out_shape=jax.ShapeDtypeStruct((n, d_model), _F32),
        compiler_params=pltpu.CompilerParams(dimension_semantics=("arbitrary",),
                                             vmem_limit_bytes=_VMEM_LIMIT),
        name="merge_mlp_prompt" if a_transposed else "merge_mlp_sample",
    )(x2d, a, mixn, wts["ag"], wts["wo_a"], wts["wo_m"], wts["ln2_g"], wts["w_up"], wts["w_down"])


def _topk_rows(gate, n_valid, k):
    nb = gate.shape[0]
    rows = lax.broadcasted_iota(jnp.int32, gate.shape, 0)
    valid = rows < n_valid
    g = jnp.where(valid, gate, -jnp.inf)
    sel = jnp.zeros(gate.shape, jnp.bool_)
    for _ in range(min(k, nb)):
        best = jnp.max(g, axis=0, keepdims=True)
        first = jnp.min(jnp.where(g == best, rows, nb), axis=0, keepdims=True)
        hit = rows == first
        sel = jnp.logical_or(sel, hit)
        g = jnp.where(hit, -jnp.inf, g)
    return jnp.logical_and(sel, valid)


def _moba_prompt_kernel(qt_ref, qa_ref, ka_ref, va_ref, ksum_ref, o_ref, sel_ref, m_ref, acc_ref):
    i = pl.program_id(2)
    tq = qt_ref.shape[3]
    qt = qt_ref[0, 0]
    row_head = lax.broadcasted_iota(jnp.int32, qt.shape, 0) // HEAD_DIM
    krow = lax.broadcasted_iota(jnp.int32, (MOBA_BLOCK, tq), 0)
    qcol = lax.broadcasted_iota(jnp.int32, (MOBA_BLOCK, tq), 1)
    causal = krow <= qcol
    kmean = ksum_ref[0, 0] * (1.0 / MOBA_BLOCK)

    for r in range(2):
        gate = jnp.dot(kmean, jnp.where(row_head == r, qt, 0.0), precision=lax.Precision.HIGHEST,
                       preferred_element_type=_F32)
        sel_ref[r] = _topk_rows(gate, i, MOBA_TOPK).astype(_F32)
        t = jnp.where(causal, _dot(ka_ref[0, r, i], qa_ref[0, r]), _NEG)
        m = jnp.max(t, axis=0, keepdims=True)
        m_ref[r] = m
        acc_ref[r] = _dot(va_ref[0, r, i], jnp.exp2(t - m).astype(_BF16))

    def past_group(g, carry):
        n0 = g * _KV_GROUP
        for r in range(2):
            qa = qa_ref[0, r]
            m = m_ref[r]
            m_new = m
            scores = []
            for d in range(_KV_GROUP):
                t = _dot(ka_ref[0, r, n0 + d], qa)
                chosen = sel_ref[r, pl.ds(n0 + d, 1), :] > 0.5
                m_new = jnp.maximum(m_new, jnp.where(chosen, jnp.max(t, axis=0, keepdims=True), _NEG))
                scores.append((t, chosen))
            pv = None
            for d, (t, chosen) in enumerate(scores):
                p = jnp.exp2(t - jnp.where(chosen, m_new, -_NEG)).astype(_BF16)
                term = _dot(va_ref[0, r, n0 + d], p)
                pv = term if pv is None else pv + term
            acc_ref[r] = jnp.exp2(m - m_new) * acc_ref[r] + pv
            m_ref[r] = m_new
        return carry

    lax.fori_loop(0, lax.div(i + (_KV_GROUP - 1), _KV_GROUP), past_group, 0)
    for r in range(2):
        acc = acc_ref[r]
        o_ref[0, 0, r * HEAD_DIM:(r + 1) * HEAD_DIM, :] = (
            acc[0:HEAD_DIM] / acc[HEAD_DIM:HEAD_DIM + 1])


def _moba_prompt(qt, qa, ka, va, ksum):
    batch, n_pair, _, seq = qt.shape
    nb = ka.shape[2]
    tq = MOBA_BLOCK
    assert nb % _KV_GROUP == 0
    return pl.pallas_call(
        _moba_prompt_kernel,
        grid=(batch, n_pair, nb),
        in_specs=[
            pl.BlockSpec((1, 1, 128, tq), lambda b, p, i: (b, p, 0, i)),
            pl.BlockSpec((1, 2, 128, tq), lambda b, p, i: (b, p, 0, i)),
            pl.BlockSpec((1, 2, nb, MOBA_BLOCK, 128), lambda b, p, i: (b, p, 0, 0, 0)),
            pl.BlockSpec((1, 2, nb, _V_ROWS, MOBA_BLOCK), lambda b, p, i: (b, p, 0, 0, 0)),
            pl.BlockSpec((1, 1, nb, 128), lambda b, p, i: (b, p, 0, 0)),
        ],
        out_specs=pl.BlockSpec((1, 1, 128, tq), lambda b, p, i: (b, p, 0, i)),
        out_shape=jax.ShapeDtypeStruct((batch, n_pair, 128, seq), _F32),
        scratch_shapes=[pltpu.VMEM((2, nb, tq), _F32),
                        pltpu.VMEM((2, 1, tq), _F32),
                        pltpu.VMEM((2, _V_ROWS, tq), _F32)],
        compiler_params=pltpu.CompilerParams(
            dimension_semantics=("arbitrary", "arbitrary", "arbitrary"),
            vmem_limit_bytes=_VMEM_LIMIT),
        name="moba_prompt",
    )(qt, qa, ka, va, ksum)


_RING = 32
_AHEAD = 24


def _moba_sample_kernel(pt_ref, q_ref, kn_ref, vn_ref, tok_ref, expand_ref, alibi_ref, newbias_ref,
                        ck_ref, cv_ref, o_ref, ring_ref, sem_ref, ksum_ref, s_ref, *, n_heads):
    b = pl.program_id(0)
    n_seq = pl.num_programs(0)
    n_pages = pt_ref.shape[1]
    tn = q_ref.shape[0]
    page_lanes = PAGE_SIZE * n_heads
    ppb = MOBA_BLOCK // PAGE_SIZE
    nb = n_pages // ppb
    steps = 2 * n_pages
    total = n_seq * steps
    g0 = b * steps

    def copy(src_ref, page, slot):
        return pltpu.make_async_copy(src_ref.at[page], ring_ref.at[slot], sem_ref.at[slot])

    def fetch(g):
        @pl.when(g < total)
        def _():
            seq = g // steps
            s = g % steps
            slot = g % _RING

            @pl.when(s < n_pages)
            def _():
                copy(ck_ref, pt_ref[seq, s], slot).start()

            @pl.when(s >= n_pages)
            def _():
                copy(cv_ref, pt_ref[seq, s - n_pages], slot).start()

    def wait(g):
        slot = g % _RING
        copy(ck_ref, 0, slot).wait()
        return slot

    @pl.when(b == 0)
    def _():
        for g in range(_AHEAD):
            fetch(jnp.int32(g))

    q = q_ref[...]
    qall_f32 = jnp.concatenate([q[:, h * HEAD_DIM:(h + 1) * HEAD_DIM] for h in range(n_heads)],
                               axis=0)
    qall = (qall_f32 * HEAD_DIM ** -0.5).astype(_BF16)
    lane_head = lax.broadcasted_iota(jnp.int32, (tn, page_lanes), 1) % n_heads

    def diagonal(prod):
        width = prod.shape[1]
        pick = lane_head[:, :width]
        out = prod[(n_heads - 1) * tn:]
        for h in range(n_heads - 2, -1, -1):
            out = jnp.where(pick == h, prod[h * tn:(h + 1) * tn], out)
        return out

    def spread(vals):
        pick = lane_head[:, :vals.shape[1]]
        return jnp.concatenate([jnp.where(pick == h, vals, 0.0) for h in range(n_heads)], axis=0)

    def k_block(n, carry):
        ksum = jnp.zeros((n_heads, HEAD_DIM), _F32)
        for r in range(ppb):
            j = n * ppb + r
            g = g0 + j
            slot = wait(g)
            kp = ring_ref[slot]
            ksum = ksum + jnp.sum(kp, axis=0)
            prod = lax.dot_general(qall, kp.reshape(page_lanes, HEAD_DIM).astype(_BF16), _NT,
                                   preferred_element_type=_F32)
            s_ref[j] = diagonal(prod)
            fetch(g + _AHEAD)
        ksum_ref[n] = ksum
        return carry

    lax.fori_loop(0, nb, k_block, 0)

    kmean = ksum_ref[...].reshape(nb * n_heads, HEAD_DIM) * (1.0 / MOBA_BLOCK)
    gate = lax.dot_general(kmean, qall_f32, _NT, precision=lax.Precision.HIGHEST,
                           preferred_element_type=_F32).reshape(nb, n_heads, n_heads * tn)
    sel = _topk_rows(gate, nb, MOBA_TOPK)
    same_head = (lax.broadcasted_iota(jnp.int32, sel.shape, 1)
                 == lax.broadcasted_iota(jnp.int32, sel.shape, 2) // tn)
    chosen = jnp.where(jnp.logical_and(sel, same_head), 1.0, 0.0).reshape(nb * n_heads, n_heads * tn)
    chosen_t = lax.dot_general(tok_ref[...], chosen.astype(_BF16), _NT, preferred_element_type=_F32)
    keep = _dot(chosen_t.astype(_BF16), expand_ref[...])

    slope = alibi_ref[0:1, :]
    base = alibi_ref[pl.ds(8, tn), :]
    run_max = jnp.full((tn, 128), _NEG, _F32)
    for n in range(nb):
        keep_n = jnp.concatenate([keep[:, n * 128:(n + 1) * 128]] * (page_lanes // 128), axis=1) > 0.5
        for r in range(ppb):
            j = n * ppb + r
            logit = jnp.where(keep_n, s_ref[j] + (base + slope * float(j * PAGE_SIZE)), _NEG)
            s_ref[j] = logit
            for c in range(page_lanes // 128):
                run_max = jnp.maximum(run_max, logit[:, c * 128:(c + 1) * 128])
    new_w = n_heads * tn
    logit_new = diagonal(lax.dot_general(qall, kn_ref[0].astype(_BF16), _NT,
                                         preferred_element_type=_F32)) + newbias_ref[...]
    run_max = jnp.maximum(run_max, jnp.concatenate(
        [logit_new, jnp.full((tn, 128 - new_w), _NEG, _F32)], axis=1))

    def over_heads(x, op):
        shift = n_heads
        while shift < 128:
            x = op(x, pltpu.roll(x, shift, axis=1))
            shift *= 2
        return x

    m = over_heads(run_max, jnp.maximum)
    m_page = jnp.concatenate([m] * (page_lanes // 128), axis=1)
    run_sum = jnp.zeros((tn, 128), _F32)
    for j in range(n_pages):
        p = jnp.exp(s_ref[j] - m_page)
        s_ref[j] = p
        for c in range(page_lanes // 128):
            run_sum = run_sum + p[:, c * 128:(c + 1) * 128]
    p_new = jnp.exp(logit_new - m[:, :new_w])
    run_sum = run_sum + jnp.concatenate([p_new, jnp.zeros((tn, 128 - new_w), _F32)], axis=1)
    inv = 1.0 / over_heads(run_sum, jnp.add)
    inv_page = jnp.concatenate([inv] * (page_lanes // 128), axis=1)

    def v_page(j, acc):
        g = g0 + n_pages + j
        slot = wait(g)
        vp = ring_ref[slot].reshape(page_lanes, HEAD_DIM).astype(_BF16)
        acc = acc + _dot(spread(s_ref[j] * inv_page).astype(_BF16), vp)
        fetch(g + _AHEAD)
        return acc

    acc = _dot(spread(p_new * inv[:, :new_w]).astype(_BF16), vn_ref[0].astype(_BF16))
    acc = lax.fori_loop(0, n_pages, v_page, acc)
    o_ref[...] = jnp.concatenate([acc[h * tn:(h + 1) * tn, :] for h in range(n_heads)], axis=1)


def _sample_tables(slopes, tn, n_pages):
    n_heads = slopes.shape[0]
    nb = n_pages * PAGE_SIZE // MOBA_BLOCK
    past = n_pages * PAGE_SIZE
    tok = (jnp.arange(tn)[:, None] == jnp.arange(n_heads * tn)[None, :] % tn).astype(_BF16)
    src = jnp.arange(nb * n_heads)
    dst = jnp.arange(nb * 128)
    expand = jnp.logical_and(src[:, None] // n_heads == dst[None, :] // 128,
                             src[:, None] % n_heads == dst[None, :] % n_heads).astype(_BF16)
    lane = jnp.arange(PAGE_SIZE * n_heads)
    slope_lane = slopes[lane % n_heads]
    tok_f = jnp.arange(tn, dtype=_F32)[:, None]
    base = slope_lane[None, :] * ((lane // n_heads).astype(_F32)[None, :] - float(past) - tok_f)
    alibi = jnp.concatenate([jnp.broadcast_to(slope_lane[None, :], (8, lane.shape[0])), base], axis=0)
    new_lane = jnp.arange(n_heads * tn)
    dist = tok_f - (new_lane // n_heads).astype(_F32)[None, :]
    newbias = jnp.where(dist >= 0.0, -slopes[new_lane % n_heads][None, :] * dist, _NEG)
    return tok, expand, alibi, newbias


def _moba_sample(q, k_new, v_new, cache_k, cache_v, pages, tables, tn):
    n, att_w = q.shape
    n_heads = att_w // HEAD_DIM
    n_seq, n_pages = pages.shape
    nb = n_pages * PAGE_SIZE // MOBA_BLOCK
    row = lambda b, pt: (b, 0)
    new_rows = lambda b, pt: (b, 0, 0)
    whole = lambda b, pt: (0, 0)
    return pl.pallas_call(
        functools.partial(_moba_sample_kernel, n_heads=n_heads),
        grid_spec=pltpu.PrefetchScalarGridSpec(
            num_scalar_prefetch=1,
            grid=(n_seq,),
            in_specs=[
                pl.BlockSpec((tn, att_w), row),
                pl.BlockSpec((1, tn * n_heads, HEAD_DIM), new_rows),
                pl.BlockSpec((1, tn * n_heads, HEAD_DIM), new_rows),
                *(pl.BlockSpec(t.shape, whole) for t in tables),
                pl.BlockSpec(memory_space=pl.ANY),
                pl.BlockSpec(memory_space=pl.ANY),
            ],
            out_specs=pl.BlockSpec((tn, att_w), row),
            scratch_shapes=[
                pltpu.VMEM((_RING, PAGE_SIZE, n_heads, HEAD_DIM), _F32),
                pltpu.SemaphoreType.DMA((_RING,)),
                pltpu.VMEM((nb, n_heads, HEAD_DIM), _F32),
                pltpu.VMEM((n_pages, tn, PAGE_SIZE * n_heads), _F32),
            ],
        ),
        out_shape=jax.ShapeDtypeStruct((n, att_w), _F32),
        compiler_params=pltpu.CompilerParams(dimension_semantics=("arbitrary",),
                                             vmem_limit_bytes=_VMEM_LIMIT),
        name="moba_sample",
    )(pages, q, k_new, v_new, *tables, cache_k, cache_v)


def _alibi_slopes(n_heads):
    h = jnp.arange(1, n_heads + 1, dtype=_F32)
    return jnp.exp2(-8.0 * h / n_heads)


def _split_terms(x):
    terms = []
    for _ in range(_POS_TERMS):
        bits = lax.bitcast_convert_type(x, jnp.uint32) & jnp.uint32(0xFFFF0000)
        t = lax.bitcast_convert_type(bits, _F32)
        terms.append(t.astype(_BF16))
        x = x - t
    return terms


def _position_tables(slopes, seq):
    n_heads = slopes.shape[0]
    pos = jnp.arange(seq, dtype=_F32)
    val = (slopes * _LOG2E)[:, None] * pos[None, :]
    ones = jnp.ones((n_heads, seq), _BF16)
    pad = [jnp.zeros((n_heads, seq), _BF16)] * (HEAD_DIM - 2 * _POS_TERMS)
    k_half = jnp.stack(_split_terms(val) + [ones] * _POS_TERMS + pad, axis=-1)
    q_half = jnp.stack([ones] * _POS_TERMS + _split_terms(-val) + pad, axis=-1)
    zero = jnp.zeros_like(k_half)
    odd = (jnp.arange(n_heads) % 2 == 1)[:, None, None]
    k_tab = jnp.where(odd, jnp.concatenate([k_half, zero], -1), jnp.concatenate([zero, k_half], -1))
    q_tab = jnp.where(odd, jnp.concatenate([q_half, zero], -1), jnp.concatenate([zero, q_half], -1))
    kpos = k_tab.reshape(n_heads, seq // MOBA_BLOCK, MOBA_BLOCK, 128)
    qpos = q_tab.transpose(0, 2, 1)
    return kpos, qpos


def kernel(x_prompt, x_sample, cache_k, cache_v, page_table, ln1_g, w_in, q_norm_g, k_norm_g,
           gm_ln_g, gm_ln_b, w_s, b_s, att_out_g, gm_out_g, w_out, ln2_g, w_up, w_down):
    batch, seq, d_model = x_prompt.shape
    dec_batch, tn, _ = x_sample.shape
    depth, n_pool, page_size, n_heads, head_dim = cache_k.shape
    gm_heads, gm_dim = gm_ln_g.shape[1:]
    att_w = n_heads * head_dim
    gm_w = gm_heads * gm_dim
    n_pages = page_table.shape[1]
    assert (head_dim, gm_dim, page_size, w_s.shape[-1]) == (HEAD_DIM, GM_HEAD_DIM, PAGE_SIZE, CHUNK)
    assert att_w == gm_w and att_w % 128 == 0 and n_heads * tn <= 128 and 128 % n_heads == 0
    assert seq % MOBA_BLOCK == 0 and CHUNK % tn == 0
    assert (n_pages * PAGE_SIZE) % MOBA_BLOCK == 0 and n_pages * PAGE_SIZE >= MOBA_BLOCK
    tm = min(_ROW_TILE, seq)
    tm_s = min(_ROW_TILE, dec_batch * tn)
    assert seq % tm == 0 and tm % MOBA_BLOCK == 0 and (dec_batch * tn) % tm_s == 0 and tm_s % CHUNK == 0
    n_pair = att_w // 128
    nb = seq // MOBA_BLOCK

    slopes = _alibi_slopes(n_heads)
    kpos, qpos = _position_tables(slopes, seq)
    group = jnp.arange(att_w) // HEAD_DIM
    gsum = (group[:, None] == group[None, :]).astype(_BF16)
    sample_tables = _sample_tables(slopes, tn, n_pages)

    xp = x_prompt.reshape(batch * seq, d_model)
    xs = x_sample.reshape(dec_batch * tn, d_model)
    ck = cache_k.reshape(depth * n_pool, page_size, n_heads, head_dim)
    cv = cache_v.reshape(depth * n_pool, page_size, n_heads, head_dim)
    outs = [[] for _ in range(5)]
    for l in range(depth):
        wts = dict(
            ln1_g=ln1_g[l][None], w_in=w_in[l].astype(_BF16),
            qg=q_norm_g[l].reshape(1, att_w), kg=k_norm_g[l].reshape(1, att_w),
            glg=gm_ln_g[l].reshape(1, gm_w), glb=gm_ln_b[l].reshape(1, gm_w),
            gsum=gsum, gmo=gm_out_g[l][None], ag=att_out_g[l][None],
            wo_a=w_out[l][:att_w].astype(_BF16), wo_m=w_out[l][att_w:].astype(_BF16),
            ln2_g=ln2_g[l][None], w_up=w_up[l].astype(_BF16), w_down=w_down[l].astype(_BF16))
        w_tril = jnp.tril(w_s[l])
        wmix_p = w_tril.transpose(1, 0, 2).reshape(CHUNK, gm_heads * CHUNK).astype(_BF16)
        bmix_p = jnp.repeat(b_s[l].T, GM_HEAD_DIM, axis=1)
        eye = jnp.eye(CHUNK // tn, dtype=_F32)
        w_blk = jnp.einsum("ab,hij->haibj", eye, w_tril[:, :tn, :tn]).reshape(gm_heads, CHUNK, CHUNK)
        wmix_s = w_blk.transpose(1, 0, 2).reshape(CHUNK, gm_heads * CHUNK).astype(_BF16)
        bmix_s = jnp.tile(jnp.repeat(b_s[l][:, :tn].T, GM_HEAD_DIM, axis=1), (CHUNK // tn, 1))

        k_p, v_p, qt, qa, ka, va, ksum, mixn_p = _project(xp, seq, wts, wmix_p, bmix_p, True, tm,
                                                          (kpos, qpos))
        ksum = ksum.reshape(batch, nb, n_pair, 128).transpose(0, 2, 1, 3)
        a_t = _moba_prompt(qt, qa, ka, va, ksum)
        xp = _merge_mlp(xp, a_t.reshape(batch, att_w, seq), mixn_p, wts, seq, tm, True)

        k_s, v_s, q_s, g_s, mixn_s = _project(xs, tn, wts, wmix_s, bmix_s, False, tm_s)
        new_rows = (dec_batch, tn * n_heads, head_dim)
        a_s = _moba_sample(q_s, k_s.reshape(new_rows), v_s.reshape(new_rows), ck, cv,
                           page_table + l * n_pool, sample_tables, tn)
        xs = _merge_mlp(xs, a_s, mixn_s, wts, tn, tm_s, False)

        outs[0].append(k_p.reshape(batch, seq, n_heads, head_dim))
        outs[1].append(v_p.reshape(batch, seq, n_heads, head_dim))
        outs[2].append(k_s.reshape(dec_batch, tn, n_heads, head_dim))
        outs[3].append(v_s.reshape(dec_batch, tn, n_heads, head_dim))
        outs[4].append(g_s.reshape(dec_batch, tn, gm_heads, gm_dim))
    return (xp.reshape(batch, seq, d_model), xs.reshape(dec_batch, tn, d_model),
            *(jnp.stack(o) for o in outs))
```

```python
import functools

import jax
import jax.numpy as jnp
from jax import lax
from jax.experimental import pallas as pl
from jax.experimental.pallas import tpu as pltpu

HEAD_DIM = 64
GM_HEAD_DIM = 64
MOBA_BLOCK = 256
MOBA_TOPK = 3
CHUNK = 128
PAGE_SIZE = 128
NORM_EPS = 1e-6

_F32 = jnp.float32
_BF16 = jnp.bfloat16
_NEG = -1e30
_VMEM_LIMIT = 56 * 1024 * 1024
_ROW_TILE = 512
_LOG2E = 1.4426950408889634
_V_ROWS = 80
_POS_TERMS = 3
_KV_GROUP = 2

_NT = (((1,), (1,)), ((), ()))
_TN = (((0,), (0,)), ((), ()))


def _dot(a, b):
    return jnp.dot(a, b, preferred_element_type=_F32)


def _split_bf16(x):
    hi = x.astype(_BF16)
    lo = (x - hi.astype(_F32)).astype(_BF16)
    return hi, lo


def _group_mean(x, gsum, width):
    hi, lo = _split_bf16(x)
    return (_dot(hi, gsum) + _dot(lo, gsum)) * (1.0 / width)


def _rms_rows(x, g):
    return x * lax.rsqrt(jnp.mean(x * x, axis=-1, keepdims=True) + NORM_EPS) * g


def _proj_kernel(x_ref, ln_g_ref, w_ref, qg_ref, kg_ref, glg_ref, glb_ref, gsum_ref,
                 wmix_ref, bmix_ref, gmo_ref, *refs, att_w, gm_w, prompt):
    if prompt:
        (kpos_ref, qpos_ref,
         k_ref, v_ref, qt_ref, qa_ref, ka_ref, va_ref, ksum_ref, mixn_ref) = refs
    else:
        k_ref, v_ref, q_ref, g_ref, mixn_ref = refs
    tm = x_ref.shape[0]
    gsum = gsum_ref[...]

    xn = _rms_rows(x_ref[...], ln_g_ref[...]).astype(_BF16)

    def zcols(lo, width):
        return _dot(xn, w_ref[:, lo:lo + width])

    def head_rms(z, g):
        return z * lax.rsqrt(_group_mean(z * z, gsum, HEAD_DIM) + NORM_EPS) * g

    q = head_rms(zcols(0, att_w), qg_ref[...])
    k = head_rms(zcols(att_w, att_w), kg_ref[...])
    v = zcols(2 * att_w, att_w)
    k_ref[...] = k
    v_ref[...] = v
    if prompt:
        n_pair = att_w // 128
        qt = q.T
        qt_ref[0] = qt.reshape(n_pair, 128, tm)
        qtb = (qt * (HEAD_DIM ** -0.5 * _LOG2E)).astype(_BF16)
        kb = k.astype(_BF16)
        vtb = v.T.astype(_BF16)
        lane_half = lax.broadcasted_iota(jnp.int32, (MOBA_BLOCK, 128), 1) // HEAD_DIM
        row_half = lax.broadcasted_iota(jnp.int32, (128, tm), 0) // HEAD_DIM
        ones_row = (lax.broadcasted_iota(jnp.int32, (_V_ROWS - HEAD_DIM, MOBA_BLOCK), 0) == 0
                    ).astype(_BF16)
        for h in range(att_w // HEAD_DIM):
            pair = slice((h // 2) * 128, (h // 2 + 1) * 128)
            qa_ref[0, h] = jnp.where(row_half == h % 2, qtb[pair], qpos_ref[h])
            for j in range(tm // MOBA_BLOCK):
                blk = slice(j * MOBA_BLOCK, (j + 1) * MOBA_BLOCK)
                ka_ref[0, h, j] = jnp.where(lane_half == h % 2, kb[blk, pair], kpos_ref[h, j])
                va_ref[0, h, j, 0:HEAD_DIM, :] = vtb[h * HEAD_DIM:(h + 1) * HEAD_DIM, blk]
                va_ref[0, h, j, HEAD_DIM:_V_ROWS, :] = ones_row
        for j in range(tm // MOBA_BLOCK):
            blk = slice(j * MOBA_BLOCK, (j + 1) * MOBA_BLOCK)
            ksum_ref[0, j:j + 1, :] = jnp.sum(k[blk], axis=0, keepdims=True)
    else:
        q_ref[...] = q

    u = jax.nn.gelu(zcols(3 * att_w, gm_w))
    gg = jax.nn.gelu(zcols(3 * att_w + gm_w, gm_w))
    mu = _group_mean(gg, gsum, GM_HEAD_DIM)
    cen = gg - mu
    var = _group_mean(cen * cen, gsum, GM_HEAD_DIM)
    g = cen * lax.rsqrt(var + NORM_EPS) * glg_ref[...] + glb_ref[...]
    if not prompt:
        g_ref[...] = g

    n_grp = gm_w // GM_HEAD_DIM
    gb = g.astype(_BF16)
    wmix = wmix_ref[...]
    bmix = bmix_ref[...]
    lane_grp = lax.broadcasted_iota(jnp.int32, (CHUNK, gm_w), 1) // GM_HEAD_DIM
    pieces = []
    for c in range(tm // CHUNK):
        rows = slice(c * CHUNK, (c + 1) * CHUNK)
        stack = jnp.concatenate(
            [jnp.where(lane_grp == h, gb[rows], jnp.zeros_like(gb[rows])) for h in range(n_grp)],
            axis=0)
        mixed = _dot(wmix, stack) + bmix
        pieces.append(u[rows] * mixed)
    m = jnp.concatenate(pieces, axis=0)
    mixn_ref[...] = _rms_rows(m, gmo_ref[...]).astype(_BF16)


def _project(x2d, seq, wts, wmix, bmix, prompt, tm, tables=()):
    n, d_model = x2d.shape
    att_w = wts["qg"].shape[1]
    gm_w = wts["glg"].shape[1]
    steps = n // tm
    per_seq = seq // tm if prompt else 1
    row = lambda i: (i, 0)
    full2 = lambda i: (0, 0)
    in_specs = [
        pl.BlockSpec((tm, d_model), row),
        pl.BlockSpec((1, d_model), full2),
        pl.BlockSpec(wts["w_in"].shape, full2),
        pl.BlockSpec((1, att_w), full2),
        pl.BlockSpec((1, att_w), full2),
        pl.BlockSpec((1, gm_w), full2),
        pl.BlockSpec((1, gm_w), full2),
        pl.BlockSpec(wts["gsum"].shape, full2),
        pl.BlockSpec(wmix.shape, full2),
        pl.BlockSpec(bmix.shape, full2),
        pl.BlockSpec((1, gm_w), full2),
    ]
    f32_rows = jax.ShapeDtypeStruct((n, att_w), _F32)
    if prompt:
        batch = n // seq
        n_pair = att_w // 128
        n_heads = att_w // HEAD_DIM
        blk_per_tile = tm // MOBA_BLOCK
        nb = seq // MOBA_BLOCK
        in_specs += [
            pl.BlockSpec((n_heads, blk_per_tile, MOBA_BLOCK, 128), lambda i: (0, i % per_seq, 0, 0)),
            pl.BlockSpec((n_heads, 128, tm), lambda i: (0, 0, i % per_seq)),
        ]
        cols = lambda i: (i // per_seq, 0, 0, i % per_seq)
        blocks = lambda i: (i // per_seq, 0, i % per_seq, 0, 0)
        out_shape = (
            f32_rows, f32_rows,
            jax.ShapeDtypeStruct((batch, n_pair, 128, seq), _F32),
            jax.ShapeDtypeStruct((batch, n_heads, 128, seq), _BF16),
            jax.ShapeDtypeStruct((batch, n_heads, nb, MOBA_BLOCK, 128), _BF16),
            jax.ShapeDtypeStruct((batch, n_heads, nb, _V_ROWS, MOBA_BLOCK), _BF16),
            jax.ShapeDtypeStruct((steps, blk_per_tile, att_w), _F32),
            jax.ShapeDtypeStruct((n, gm_w), _BF16),
        )
        out_specs = (
            pl.BlockSpec((tm, att_w), row), pl.BlockSpec((tm, att_w), row),
            pl.BlockSpec((1, n_pair, 128, tm), cols),
            pl.BlockSpec((1, n_heads, 128, tm), cols),
            pl.BlockSpec((1, n_heads, blk_per_tile, MOBA_BLOCK, 128), blocks),
            pl.BlockSpec((1, n_heads, blk_per_tile, _V_ROWS, MOBA_BLOCK), blocks),
            pl.BlockSpec((1, blk_per_tile, att_w), lambda i: (i, 0, 0)),
            pl.BlockSpec((tm, gm_w), row),
        )
    else:
        out_shape = (f32_rows, f32_rows, f32_rows,
                     jax.ShapeDtypeStruct((n, gm_w), _F32),
                     jax.ShapeDtypeStruct((n, gm_w), _BF16))
        out_specs = tuple(pl.BlockSpec((tm, att_w), row) for _ in range(5))
    return pl.pallas_call(
        functools.partial(_proj_kernel, att_w=att_w, gm_w=gm_w, prompt=prompt),
        grid=(steps,),
        in_specs=in_specs,
        out_specs=out_specs,
        out_shape=out_shape,
        compiler_params=pltpu.CompilerParams(dimension_semantics=("arbitrary",),
                                             vmem_limit_bytes=_VMEM_LIMIT),
        name="project_prompt" if prompt else "project_sample",
    )(x2d, wts["ln1_g"], wts["w_in"], wts["qg"], wts["kg"], wts["glg"], wts["glb"], wts["gsum"],
      wmix, bmix, wts["gmo"], *tables)


def _mlp_kernel(x_ref, a_ref, mixn_ref, ag_ref, wo_a_ref, wo_m_ref, ln2_ref, wup_ref, wdn_ref,
                o_ref, *, a_transposed, ff_chunk):
    a = a_ref[0].T if a_transposed else a_ref[...]
    an = _rms_rows(a, ag_ref[...]).astype(_BF16)
    x1 = x_ref[...] + _dot(an, wo_a_ref[...]) + _dot(mixn_ref[...], wo_m_ref[...])
    hn = _rms_rows(x1, ln2_ref[...]).astype(_BF16)
    y = None
    for c in range(wup_ref.shape[1] // ff_chunk):
        cols = slice(c * ff_chunk, (c + 1) * ff_chunk)
        h = jnp.square(jnp.maximum(_dot(hn, wup_ref[:, cols]), 0.0)).astype(_BF16)
        d = _dot(h, wdn_ref[cols, :])
        y = d if y is None else y + d
    o_ref[...] = x1 + y


def _merge_mlp(x2d, a, mixn, wts, seq, tm, a_transposed):
    n, d_model = x2d.shape
    att_w = wts["wo_a"].shape[0]
    gm_w = wts["wo_m"].shape[0]
    d_ff = wts["w_up"].shape[1]
    row = lambda i: (i, 0)
    full2 = lambda i: (0, 0)
    once = dict(pipeline_mode=pl.Buffered(1))
    if a_transposed:
        per_seq = seq // tm
        a_spec = pl.BlockSpec((1, att_w, tm), lambda i: (i // per_seq, 0, i % per_seq))
    else:
        a_spec = pl.BlockSpec((tm, att_w), row)
    return pl.pallas_call(
        functools.partial(_mlp_kernel, a_transposed=a_transposed, ff_chunk=min(d_ff, 1024)),
        grid=(n // tm,),
        in_specs=[
            pl.BlockSpec((tm, d_model), row),
            a_spec,
            pl.BlockSpec((tm, gm_w), row),
            pl.BlockSpec((1, att_w), full2),
            pl.BlockSpec((att_w, d_model), full2, **once),
            pl.BlockSpec((gm_w, d_model), full2, **once),
            pl.BlockSpec((1, d_model), full2),
            pl.BlockSpec((d_model, d_ff), full2, **once),
            pl.BlockSpec((d_ff, d_model), full2, **once),
        ],
        out_specs=pl.BlockSpec((tm, d_model), row),
        out_shape=jax.ShapeDtypeStruct((n, d_model), _F32),
        compiler_params=pltpu.CompilerParams(dimension_semantics=("arbitrary",),
                                             vmem_limit_bytes=_VMEM_LIMIT),
        name="merge_mlp_prompt" if a_transposed else "merge_mlp_sample",
    )(x2d, a, mixn, wts["ag"], wts["wo_a"], wts["wo_m"], wts["ln2_g"], wts["w_up"], wts["w_down"])


def _topk_rows(gate, n_valid, k):
    nb = gate.shape[0]
    rows = lax.broadcasted_iota(jnp.int32, gate.shape, 0)
    valid = rows < n_valid
    g = jnp.where(valid, gate, -jnp.inf)
    sel = jnp.zeros(gate.shape, jnp.bool_)
    for _ in range(min(k, nb)):
        best = jnp.max(g, axis=0, keepdims=True)
        first = jnp.min(jnp.where(g == best, rows, nb), axis=0, keepdims=True)
        hit = rows == first
        sel = jnp.logical_or(sel, hit)
        g = jnp.where(hit, -jnp.inf, g)
    return jnp.logical_and(sel, valid)


def _moba_prompt_kernel(qt_ref, qa_ref, ka_ref, va_ref, ksum_ref, o_ref, sel_ref, m_ref, acc_ref):
    i = pl.program_id(2)
    tq = qt_ref.shape[3]
    qt = qt_ref[0, 0]
    row_head = lax.broadcasted_iota(jnp.int32, qt.shape, 0) // HEAD_DIM
    krow = lax.broadcasted_iota(jnp.int32, (MOBA_BLOCK, tq), 0)
    qcol = lax.broadcasted_iota(jnp.int32, (MOBA_BLOCK, tq), 1)
    causal = krow <= qcol
    kmean = ksum_ref[0, 0] * (1.0 / MOBA_BLOCK)

    for r in range(2):
        gate = jnp.dot(kmean, jnp.where(row_head == r, qt, 0.0), precision=lax.Precision.HIGHEST,
                       preferred_element_type=_F32)
        sel_ref[r] = _topk_rows(gate, i, MOBA_TOPK).astype(_F32)
        t = jnp.where(causal, _dot(ka_ref[0, r, i], qa_ref[0, r]), _NEG)
        m = jnp.max(t, axis=0, keepdims=True)
        m_ref[r] = m
        acc_ref[r] = _dot(va_ref[0, r, i], jnp.exp2(t - m).astype(_BF16))

    def past_group(g, carry):
        n0 = g * _KV_GROUP
        for r in range(2):
            qa = qa_ref[0, r]
            m = m_ref[r]
            m_new = m
            scores = []
            for d in range(_KV_GROUP):
                t = _dot(ka_ref[0, r, n0 + d], qa)
                chosen = sel_ref[r, pl.ds(n0 + d, 1), :] > 0.5
                m_new = jnp.maximum(m_new, jnp.where(chosen, jnp.max(t, axis=0, keepdims=True), _NEG))
                scores.append((t, chosen))
            pv = None
            for d, (t, chosen) in enumerate(scores):
                p = jnp.exp2(t - jnp.where(chosen, m_new, -_NEG)).astype(_BF16)
                term = _dot(va_ref[0, r, n0 + d], p)
                pv = term if pv is None else pv + term
            acc_ref[r] = jnp.exp2(m - m_new) * acc_ref[r] + pv
            m_ref[r] = m_new
        return carry

    lax.fori_loop(0, lax.div(i + (_KV_GROUP - 1), _KV_GROUP), past_group, 0)
    for r in range(2):
        acc = acc_ref[r]
        o_ref[0, 0, r * HEAD_DIM:(r + 1) * HEAD_DIM, :] = (
            acc[0:HEAD_DIM] / acc[HEAD_DIM:HEAD_DIM + 1])


def _moba_prompt(qt, qa, ka, va, ksum):
    batch, n_pair, _, seq = qt.shape
    nb = ka.shape[2]
    tq = MOBA_BLOCK
    assert nb % _KV_GROUP == 0
    return pl.pallas_call(
        _moba_prompt_kernel,
        grid=(batch, n_pair, nb),
        in_specs=[
            pl.BlockSpec((1, 1, 128, tq), lambda b, p, i: (b, p, 0, i)),
            pl.BlockSpec((1, 2, 128, tq), lambda b, p, i: (b, p, 0, i)),
            pl.BlockSpec((1, 2, nb, MOBA_BLOCK, 128), lambda b, p, i: (b, p, 0, 0, 0)),
            pl.BlockSpec((1, 2, nb, _V_ROWS, MOBA_BLOCK), lambda b, p, i: (b, p, 0, 0, 0)),
            pl.BlockSpec((1, 1, nb, 128), lambda b, p, i: (b, p, 0, 0)),
        ],
        out_specs=pl.BlockSpec((1, 1, 128, tq), lambda b, p, i: (b, p, 0, i)),
        out_shape=jax.ShapeDtypeStruct((batch, n_pair, 128, seq), _F32),
        scratch_shapes=[pltpu.VMEM((2, nb, tq), _F32),
                        pltpu.VMEM((2, 1, tq), _F32),
                        pltpu.VMEM((2, _V_ROWS, tq), _F32)],
        compiler_params=pltpu.CompilerParams(
            dimension_semantics=("arbitrary", "arbitrary", "arbitrary"),
            vmem_limit_bytes=_VMEM_LIMIT),
        name="moba_prompt",
    )(qt, qa, ka, va, ksum)


_RING = 32
_AHEAD = 24
_PAGE_GROUP = 8


def _moba_sample_kernel(pt_ref, q_ref, kn_ref, vn_ref, alibi_ref, newbias_ref,
                        ck_ref, cv_ref, o_ref, kring_ref, vring_ref, ksem_ref, vsem_ref,
                        s_ref, *, n_heads):
    b = pl.program_id(0)
    n_seq = pl.num_programs(0)
    n_pages = pt_ref.shape[1]
    tn, att_w = q_ref.shape
    q_rows = n_heads * tn
    ppb = MOBA_BLOCK // PAGE_SIZE
    nb = n_pages // ppb
    i0 = b * n_pages

    def copy(src_ref, ring_ref, sem_ref, page, slot):
        return pltpu.make_async_copy(src_ref.at[page], ring_ref.at[slot], sem_ref.at[slot])

    def fetch(src_ref, ring_ref, sem_ref, i):
        seq = jnp.minimum(i // n_pages, n_seq - 1)
        copy(src_ref, ring_ref, sem_ref, pt_ref[seq, i % n_pages], i % _RING).start()

    def wait(src_ref, ring_ref, sem_ref, i):
        copy(src_ref, ring_ref, sem_ref, 0, i % _RING).wait()

    k_stream = (ck_ref, kring_ref, ksem_ref)
    v_stream = (cv_ref, vring_ref, vsem_ref)

    @pl.when(b == 0)
    def _():
        for i in range(_AHEAD):
            fetch(*k_stream, jnp.int32(i))
            fetch(*v_stream, jnp.int32(i))

    q = q_ref[...]
    qrow = lax.broadcasted_iota(jnp.int32, (q_rows, att_w), 0)
    qlane = lax.broadcasted_iota(jnp.int32, (q_rows, att_w), 1)
    qbd = jnp.where(qrow // tn == qlane // HEAD_DIM, jnp.concatenate([q] * n_heads, axis=0), 0.0)
    qbd = (qbd * HEAD_DIM ** -0.5).astype(_BF16)
    lane = lax.broadcasted_iota(jnp.int32, (q_rows, 128), 1)

    def k_pages(step, gate):
        first = step * _PAGE_GROUP
        for u in range(_PAGE_GROUP):
            wait(*k_stream, i0 + first + u)
        for u in range(_PAGE_GROUP):
            slot = (i0 + first + u) % _RING
            s = _dot(qbd, kring_ref[slot].astype(_BF16))
            s_ref[first + u] = s
            gate = gate + jnp.where(lane == (first + u) // ppb,
                                    jnp.sum(s, axis=1, keepdims=True), 0.0)
        for u in range(_PAGE_GROUP):
            fetch(*k_stream, i0 + first + u + _AHEAD)
        return gate

    gate = lax.fori_loop(0, n_pages // _PAGE_GROUP, k_pages, jnp.zeros((q_rows, 128), _F32))

    gate = jnp.where(lane < nb, gate, -jnp.inf)
    sel = jnp.zeros(gate.shape, jnp.bool_)
    for _ in range(min(MOBA_TOPK, nb)):
        best = jnp.max(gate, axis=1, keepdims=True)
        first = jnp.min(jnp.where(gate == best, lane, 128), axis=1, keepdims=True)
        hit = lane == first
        sel = jnp.logical_or(sel, hit)
        gate = jnp.where(hit, -jnp.inf, gate)
    sel = jnp.where(jnp.logical_and(sel, lane < nb), 1.0, 0.0)

    slope = alibi_ref[0]
    origin = alibi_ref[1]

    def logits_pages(step, run_max):
        for u in range(_PAGE_GROUP):
            j = step * _PAGE_GROUP + u
            pos = (lane + j * PAGE_SIZE).astype(_F32)
            chosen = jnp.sum(jnp.where(lane == j // ppb, sel, 0.0), axis=1, keepdims=True) > 0.5
            logit = jnp.where(chosen, s_ref[j] + (slope * pos + origin), _NEG)
            s_ref[j] = logit
            run_max = jnp.maximum(run_max, logit)
        return run_max

    run_max = lax.fori_loop(0, n_pages // _PAGE_GROUP, logits_pages,
                            jnp.full((q_rows, 128), _NEG, _F32))
    logit_new = lax.dot_general(qbd, kn_ref[...].astype(_BF16), _NT,
                                preferred_element_type=_F32) + newbias_ref[...]
    m = jnp.maximum(jnp.max(run_max, axis=1, keepdims=True), jnp.max(logit_new, axis=1, keepdims=True))
    m_tile = jnp.broadcast_to(m, (q_rows, 128))

    def exp_pages(step, run_sum):
        for u in range(_PAGE_GROUP):
            j = step * _PAGE_GROUP + u
            p = jnp.exp(s_ref[j] - m_tile)
            s_ref[j] = p
            run_sum = run_sum + p
        return run_sum

    run_sum = lax.fori_loop(0, n_pages // _PAGE_GROUP, exp_pages, jnp.zeros((q_rows, 128), _F32))
    p_new = jnp.exp(logit_new - m)
    inv = 1.0 / (jnp.sum(run_sum, axis=1, keepdims=True) + jnp.sum(p_new, axis=1, keepdims=True))
    inv_tile = jnp.broadcast_to(inv, (q_rows, 128))

    def v_pages(step, acc):
        first = step * _PAGE_GROUP
        for u in range(_PAGE_GROUP):
            wait(*v_stream, i0 + first + u)
        for u in range(_PAGE_GROUP):
            slot = (i0 + first + u) % _RING
            p = (s_ref[first + u] * inv_tile).astype(_BF16)
            acc = acc + lax.dot_general(p, vring_ref[slot].astype(_BF16), _NT,
                                        preferred_element_type=_F32)
        for u in range(_PAGE_GROUP):
            fetch(*v_stream, i0 + first + u + _AHEAD)
        return acc

    acc = _dot((p_new * inv).astype(_BF16), vn_ref[...].astype(_BF16))
    acc = lax.fori_loop(0, n_pages // _PAGE_GROUP, v_pages, acc)

    @pl.when(b == n_seq - 1)
    def _():
        for i in range(_AHEAD):
            wait(*k_stream, n_seq * n_pages + i)
            wait(*v_stream, n_seq * n_pages + i)

    lane_head = lax.broadcasted_iota(jnp.int32, (tn, att_w), 1) // HEAD_DIM
    out = jnp.zeros((tn, att_w), _F32)
    for h in range(n_heads):
        out = out + jnp.where(lane_head == h, acc[h * tn:(h + 1) * tn, :], 0.0)
    o_ref[...] = out


def _sample_tables(slopes, tn, n_pages):
    n_heads = slopes.shape[0]
    n_keys = n_pages * PAGE_SIZE
    row = jnp.arange(n_heads * tn)
    slope_row = slopes[row // tn][:, None]
    tok = (row % tn).astype(_F32)[:, None]
    alibi = jnp.stack([jnp.broadcast_to(slope_row, (n_heads * tn, 128)),
                       jnp.broadcast_to(-slope_row * (float(n_keys) + tok), (n_heads * tn, 128))])
    dist = tok - jnp.arange(tn, dtype=_F32)[None, :]
    newbias = jnp.where(dist >= 0.0, -slope_row * dist, _NEG)
    return alibi, newbias


def _moba_sample(q, k_new, v_new, cache_k, cache_v, pages, tables, tn):
    n, att_w = q.shape
    n_heads = att_w // HEAD_DIM
    n_seq, n_pages = pages.shape
    assert n_pages % _PAGE_GROUP == 0 and _AHEAD + _PAGE_GROUP <= _RING
    row = lambda b, pt: (b, 0)
    return pl.pallas_call(
        functools.partial(_moba_sample_kernel, n_heads=n_heads),
        grid_spec=pltpu.PrefetchScalarGridSpec(
            num_scalar_prefetch=1,
            grid=(n_seq,),
            in_specs=[
                pl.BlockSpec((tn, att_w), row),
                pl.BlockSpec((tn, att_w), row),
                pl.BlockSpec((tn, att_w), row),
                *(pl.BlockSpec(t.shape, lambda b, pt, nd=t.ndim: (0,) * nd) for t in tables),
                pl.BlockSpec(memory_space=pl.ANY),
                pl.BlockSpec(memory_space=pl.ANY),
            ],
            out_specs=pl.BlockSpec((tn, att_w), row),
            scratch_shapes=[
                pltpu.VMEM((_RING, att_w, PAGE_SIZE), _F32),
                pltpu.VMEM((_RING, att_w, PAGE_SIZE), _F32),
                pltpu.SemaphoreType.DMA((_RING,)),
                pltpu.SemaphoreType.DMA((_RING,)),
                pltpu.VMEM((n_pages, n_heads * tn, PAGE_SIZE), _F32),
            ],
        ),
        out_shape=jax.ShapeDtypeStruct((n, att_w), _F32),
        compiler_params=pltpu.CompilerParams(dimension_semantics=("arbitrary",),
                                             vmem_limit_bytes=_VMEM_LIMIT),
        name="moba_sample",
    )(pages, q, k_new, v_new, *tables, cache_k, cache_v)


def _alibi_slopes(n_heads):
    h = jnp.arange(1, n_heads + 1, dtype=_F32)
    return jnp.exp2(-8.0 * h / n_heads)


def _split_terms(x):
    terms = []
    for _ in range(_POS_TERMS):
        bits = lax.bitcast_convert_type(x, jnp.uint32) & jnp.uint32(0xFFFF0000)
        t = lax.bitcast_convert_type(bits, _F32)
        terms.append(t.astype(_BF16))
        x = x - t
    return terms


def _position_tables(slopes, seq):
    n_heads = slopes.shape[0]
    pos = jnp.arange(seq, dtype=_F32)
    val = (slopes * _LOG2E)[:, None] * pos[None, :]
    ones = jnp.ones((n_heads, seq), _BF16)
    pad = [jnp.zeros((n_heads, seq), _BF16)] * (HEAD_DIM - 2 * _POS_TERMS)
    k_half = jnp.stack(_split_terms(val) + [ones] * _POS_TERMS + pad, axis=-1)
    q_half = jnp.stack([ones] * _POS_TERMS + _split_terms(-val) + pad, axis=-1)
    zero = jnp.zeros_like(k_half)
    odd = (jnp.arange(n_heads) % 2 == 1)[:, None, None]
    k_tab = jnp.where(odd, jnp.concatenate([k_half, zero], -1), jnp.concatenate([zero, k_half], -1))
    q_tab = jnp.where(odd, jnp.concatenate([q_half, zero], -1), jnp.concatenate([zero, q_half], -1))
    kpos = k_tab.reshape(n_heads, seq // MOBA_BLOCK, MOBA_BLOCK, 128)
    qpos = q_tab.transpose(0, 2, 1)
    return kpos, qpos


def kernel(x_prompt, x_sample, cache_k, cache_v, page_table, ln1_g, w_in, q_norm_g, k_norm_g,
           gm_ln_g, gm_ln_b, w_s, b_s, att_out_g, gm_out_g, w_out, ln2_g, w_up, w_down):
    batch, seq, d_model = x_prompt.shape
    dec_batch, tn, _ = x_sample.shape
    depth, n_pool, page_size, n_heads, head_dim = cache_k.shape
    gm_heads, gm_dim = gm_ln_g.shape[1:]
    att_w = n_heads * head_dim
    gm_w = gm_heads * gm_dim
    n_pages = page_table.shape[1]
    assert (head_dim, gm_dim, page_size, w_s.shape[-1]) == (HEAD_DIM, GM_HEAD_DIM, PAGE_SIZE, CHUNK)
    assert att_w == gm_w and att_w % 128 == 0 and n_heads * tn <= 128 and 128 % n_heads == 0
    assert seq % MOBA_BLOCK == 0 and CHUNK % tn == 0
    assert (n_pages * PAGE_SIZE) % MOBA_BLOCK == 0 and n_pages * PAGE_SIZE >= MOBA_BLOCK
    tm = min(_ROW_TILE, seq)
    tm_s = min(_ROW_TILE, dec_batch * tn)
    assert seq % tm == 0 and tm % MOBA_BLOCK == 0 and (dec_batch * tn) % tm_s == 0 and tm_s % CHUNK == 0
    n_pair = att_w // 128
    nb = seq // MOBA_BLOCK

    slopes = _alibi_slopes(n_heads)
    kpos, qpos = _position_tables(slopes, seq)
    group = jnp.arange(att_w) // HEAD_DIM
    gsum = (group[:, None] == group[None, :]).astype(_BF16)
    sample_tables = _sample_tables(slopes, tn, n_pages)

    xp = x_prompt.reshape(batch * seq, d_model)
    xs = x_sample.reshape(dec_batch * tn, d_model)
    ck = cache_k.transpose(0, 1, 3, 4, 2).reshape(depth * n_pool, att_w, page_size)
    cv = cache_v.transpose(0, 1, 3, 4, 2).reshape(depth * n_pool, att_w, page_size)
    outs = [[] for _ in range(5)]
    for l in range(depth):
        wts = dict(
            ln1_g=ln1_g[l][None], w_in=w_in[l].astype(_BF16),
            qg=q_norm_g[l].reshape(1, att_w), kg=k_norm_g[l].reshape(1, att_w),
            glg=gm_ln_g[l].reshape(1, gm_w), glb=gm_ln_b[l].reshape(1, gm_w),
            gsum=gsum, gmo=gm_out_g[l][None], ag=att_out_g[l][None],
            wo_a=w_out[l][:att_w].astype(_BF16), wo_m=w_out[l][att_w:].astype(_BF16),
            ln2_g=ln2_g[l][None], w_up=w_up[l].astype(_BF16), w_down=w_down[l].astype(_BF16))
        w_tril = jnp.tril(w_s[l])
        wmix_p = w_tril.transpose(1, 0, 2).reshape(CHUNK, gm_heads * CHUNK).astype(_BF16)
        bmix_p = jnp.repeat(b_s[l].T, GM_HEAD_DIM, axis=1)
        eye = jnp.eye(CHUNK // tn, dtype=_F32)
        w_blk = jnp.einsum("ab,hij->haibj", eye, w_tril[:, :tn, :tn]).reshape(gm_heads, CHUNK, CHUNK)
        wmix_s = w_blk.transpose(1, 0, 2).reshape(CHUNK, gm_heads * CHUNK).astype(_BF16)
        bmix_s = jnp.tile(jnp.repeat(b_s[l][:, :tn].T, GM_HEAD_DIM, axis=1), (CHUNK // tn, 1))

        k_p, v_p, qt, qa, ka, va, ksum, mixn_p = _project(xp, seq, wts, wmix_p, bmix_p, True, tm,
                                                          (kpos, qpos))
        ksum = ksum.reshape(batch, nb, n_pair, 128).transpose(0, 2, 1, 3)
        a_t = _moba_prompt(qt, qa, ka, va, ksum)
        xp = _merge_mlp(xp, a_t.reshape(batch, att_w, seq), mixn_p, wts, seq, tm, True)

        k_s, v_s, q_s, g_s, mixn_s = _project(xs, tn, wts, wmix_s, bmix_s, False, tm_s)
        a_s = _moba_sample(q_s, k_s, v_s, ck, cv, page_table + l * n_pool, sample_tables, tn)
        xs = _merge_mlp(xs, a_s, mixn_s, wts, tn, tm_s, False)

        outs[0].append(k_p.reshape(batch, seq, n_heads, head_dim))
        outs[1].append(v_p.reshape(batch, seq, n_heads, head_dim))
        outs[2].append(k_s.reshape(dec_batch, tn, n_heads, head_dim))
        outs[3].append(v_s.reshape(dec_batch, tn, n_heads, head_dim))
        outs[4].append(g_s.reshape(dec_batch, tn, gm_heads, gm_dim))
    return (xp.reshape(batch, seq, d_model), xs.reshape(dec_batch, tn, d_model),
            *(jnp.stack(o) for o in outs))
```

```python
import functools

import jax
import jax.numpy as jnp
from jax import lax
from jax.experimental import pallas as pl
from jax.experimental.pallas import tpu as pltpu

HEAD_DIM = 64
GM_HEAD_DIM = 64
MOBA_BLOCK = 256
MOBA_TOPK = 3
CHUNK = 128
PAGE_SIZE = 128
NORM_EPS = 1e-6

_F32 = jnp.float32
_BF16 = jnp.bfloat16
_NEG = -1e30
_VMEM_LIMIT = 56 * 1024 * 1024
_ROW_TILE = 512
_LOG2E = 1.4426950408889634
_V_ROWS = 80
_POS_TERMS = 3

_NT = (((1,), (1,)), ((), ()))
_TN = (((0,), (0,)), ((), ()))


def _dot(a, b):
    return jnp.dot(a, b, preferred_element_type=_F32)


def _split_bf16(x):
    hi = x.astype(_BF16)
    lo = (x - hi.astype(_F32)).astype(_BF16)
    return hi, lo


def _group_mean(x, gsum, width):
    hi, lo = _split_bf16(x)
    return (_dot(hi, gsum) + _dot(lo, gsum)) * (1.0 / width)


def _rms_rows(x, g):
    return x * lax.rsqrt(jnp.mean(x * x, axis=-1, keepdims=True) + NORM_EPS) * g


def _proj_kernel(x_ref, ln_g_ref, w_ref, qg_ref, kg_ref, glg_ref, glb_ref, gsum_ref,
                 wmix_ref, bmix_ref, gmo_ref, *refs, att_w, gm_w, prompt):
    if prompt:
        (kpos_ref, qpos_ref,
         k_ref, v_ref, qt_ref, qa_ref, ka_ref, va_ref, ksum_ref, mixn_ref) = refs
    else:
        k_ref, v_ref, q_ref, g_ref, mixn_ref = refs
    tm = x_ref.shape[0]
    gsum = gsum_ref[...]

    xn = _rms_rows(x_ref[...], ln_g_ref[...]).astype(_BF16)

    def zcols(lo, width):
        return _dot(xn, w_ref[:, lo:lo + width])

    def head_rms(z, g):
        return z * lax.rsqrt(_group_mean(z * z, gsum, HEAD_DIM) + NORM_EPS) * g

    q = head_rms(zcols(0, att_w), qg_ref[...])
    k = head_rms(zcols(att_w, att_w), kg_ref[...])
    v = zcols(2 * att_w, att_w)
    if prompt:
        vt = v.T
        k_ref[0] = k.T
        v_ref[0] = vt
        n_pair = att_w // 128
        qt = q.T
        qt_ref[0] = qt.reshape(n_pair, 128, tm)
        qtb = (qt * (HEAD_DIM ** -0.5 * _LOG2E)).astype(_BF16)
        kb = k.astype(_BF16)
        vtb = vt.astype(_BF16)
        lane_half = lax.broadcasted_iota(jnp.int32, (MOBA_BLOCK, 128), 1) // HEAD_DIM
        row_half = lax.broadcasted_iota(jnp.int32, (128, tm), 0) // HEAD_DIM
        ones_row = (lax.broadcasted_iota(jnp.int32, (_V_ROWS - HEAD_DIM, MOBA_BLOCK), 0) == 0
                    ).astype(_BF16)
        for h in range(att_w // HEAD_DIM):
            pair = slice((h // 2) * 128, (h // 2 + 1) * 128)
            qa_ref[0, h] = jnp.where(row_half == h % 2, qtb[pair], qpos_ref[h])
            for j in range(tm // MOBA_BLOCK):
                blk = slice(j * MOBA_BLOCK, (j + 1) * MOBA_BLOCK)
                ka_ref[0, h, j] = jnp.where(lane_half == h % 2, kb[blk, pair], kpos_ref[h, j])
                va_ref[0, h, j, 0:HEAD_DIM, :] = vtb[h * HEAD_DIM:(h + 1) * HEAD_DIM, blk]
                va_ref[0, h, j, HEAD_DIM:_V_ROWS, :] = ones_row
        for j in range(tm // MOBA_BLOCK):
            blk = slice(j * MOBA_BLOCK, (j + 1) * MOBA_BLOCK)
            ksum_ref[0, j:j + 1, :] = jnp.sum(k[blk], axis=0, keepdims=True)
    else:
        k_ref[...] = k
        v_ref[...] = v
        q_ref[...] = q

    u = jax.nn.gelu(zcols(3 * att_w, gm_w))
    gg = jax.nn.gelu(zcols(3 * att_w + gm_w, gm_w))
    mu = _group_mean(gg, gsum, GM_HEAD_DIM)
    cen = gg - mu
    var = _group_mean(cen * cen, gsum, GM_HEAD_DIM)
    g = cen * lax.rsqrt(var + NORM_EPS) * glg_ref[...] + glb_ref[...]
    if not prompt:
        g_ref[...] = g

    n_grp = gm_w // GM_HEAD_DIM
    gb = g.astype(_BF16)
    wmix = wmix_ref[...]
    bmix = bmix_ref[...]
    lane_grp = lax.broadcasted_iota(jnp.int32, (CHUNK, gm_w), 1) // GM_HEAD_DIM
    pieces = []
    for c in range(tm // CHUNK):
        rows = slice(c * CHUNK, (c + 1) * CHUNK)
        stack = jnp.concatenate(
            [jnp.where(lane_grp == h, gb[rows], jnp.zeros_like(gb[rows])) for h in range(n_grp)],
            axis=0)
        mixed = _dot(wmix, stack) + bmix
        pieces.append(u[rows] * mixed)
    m = jnp.concatenate(pieces, axis=0)
    mixn_ref[...] = _rms_rows(m, gmo_ref[...]).astype(_BF16)


def _project(x2d, seq, wts, wmix, bmix, prompt, tm, tables=()):
    n, d_model = x2d.shape
    att_w = wts["qg"].shape[1]
    gm_w = wts["glg"].shape[1]
    steps = n // tm
    per_seq = seq // tm if prompt else 1
    row = lambda i: (i, 0)
    full2 = lambda i: (0, 0)
    in_specs = [
        pl.BlockSpec((tm, d_model), row),
        pl.BlockSpec((1, d_model), full2),
        pl.BlockSpec(wts["w_in"].shape, full2),
        pl.BlockSpec((1, att_w), full2),
        pl.BlockSpec((1, att_w), full2),
        pl.BlockSpec((1, gm_w), full2),
        pl.BlockSpec((1, gm_w), full2),
        pl.BlockSpec(wts["gsum"].shape, full2),
        pl.BlockSpec(wmix.shape, full2),
        pl.BlockSpec(bmix.shape, full2),
        pl.BlockSpec((1, gm_w), full2),
    ]
    f32_rows = jax.ShapeDtypeStruct((n, att_w), _F32)
    if prompt:
        batch = n // seq
        n_pair = att_w // 128
        n_heads = att_w // HEAD_DIM
        blk_per_tile = tm // MOBA_BLOCK
        nb = seq // MOBA_BLOCK
        in_specs += [
            pl.BlockSpec((n_heads, blk_per_tile, MOBA_BLOCK, 128), lambda i: (0, i % per_seq, 0, 0)),
            pl.BlockSpec((n_heads, 128, tm), lambda i: (0, 0, i % per_seq)),
        ]
        cols = lambda i: (i // per_seq, 0, 0, i % per_seq)
        blocks = lambda i: (i // per_seq, 0, i % per_seq, 0, 0)
        kv_t = jax.ShapeDtypeStruct((batch, att_w, seq), _F32)
        kv_t_spec = pl.BlockSpec((1, att_w, tm), lambda i: (i // per_seq, 0, i % per_seq))
        out_shape = (
            kv_t, kv_t,
            jax.ShapeDtypeStruct((batch, n_pair, 128, seq), _F32),
            jax.ShapeDtypeStruct((batch, n_heads, 128, seq), _BF16),
            jax.ShapeDtypeStruct((batch, n_heads, nb, MOBA_BLOCK, 128), _BF16),
            jax.ShapeDtypeStruct((batch, n_heads, nb, _V_ROWS, MOBA_BLOCK), _BF16),
            jax.ShapeDtypeStruct((steps, blk_per_tile, att_w), _F32),
            jax.ShapeDtypeStruct((n, gm_w), _BF16),
        )
        out_specs = (
            kv_t_spec, kv_t_spec,
            pl.BlockSpec((1, n_pair, 128, tm), cols),
            pl.BlockSpec((1, n_heads, 128, tm), cols),
            pl.BlockSpec((1, n_heads, blk_per_tile, MOBA_BLOCK, 128), blocks),
            pl.BlockSpec((1, n_heads, blk_per_tile, _V_ROWS, MOBA_BLOCK), blocks),
            pl.BlockSpec((1, blk_per_tile, att_w), lambda i: (i, 0, 0)),
            pl.BlockSpec((tm, gm_w), row),
        )
    else:
        out_shape = (f32_rows, f32_rows, f32_rows,
                     jax.ShapeDtypeStruct((n, gm_w), _F32),
                     jax.ShapeDtypeStruct((n, gm_w), _BF16))
        out_specs = tuple(pl.BlockSpec((tm, att_w), row) for _ in range(5))
    return pl.pallas_call(
        functools.partial(_proj_kernel, att_w=att_w, gm_w=gm_w, prompt=prompt),
        grid=(steps,),
        in_specs=in_specs,
        out_specs=out_specs,
        out_shape=out_shape,
        compiler_params=pltpu.CompilerParams(dimension_semantics=("arbitrary",),
                                             vmem_limit_bytes=_VMEM_LIMIT),
        name="project_prompt" if prompt else "project_sample",
    )(x2d, wts["ln1_g"], wts["w_in"], wts["qg"], wts["kg"], wts["glg"], wts["glb"], wts["gsum"],
      wmix, bmix, wts["gmo"], *tables)


def _mlp_kernel(x_ref, a_ref, mixn_ref, ag_ref, wo_a_ref, wo_m_ref, ln2_ref, wup_ref, wdn_ref,
                o_ref, *, a_transposed, ff_chunk):
    a = a_ref[0].T if a_transposed else a_ref[...]
    an = _rms_rows(a, ag_ref[...]).astype(_BF16)
    x1 = x_ref[...] + _dot(an, wo_a_ref[...]) + _dot(mixn_ref[...], wo_m_ref[...])
    hn = _rms_rows(x1, ln2_ref[...]).astype(_BF16)
    y = None
    for c in range(wup_ref.shape[1] // ff_chunk):
        cols = slice(c * ff_chunk, (c + 1) * ff_chunk)
        h = jnp.square(jnp.maximum(_dot(hn, wup_ref[:, cols]), 0.0)).astype(_BF16)
        d = _dot(h, wdn_ref[cols, :])
        y = d if y is None else y + d
    o_ref[...] = x1 + y


def _merge_mlp(x2d, a, mixn, wts, seq, tm, a_transposed):
    n, d_model = x2d.shape
    att_w = wts["wo_a"].shape[0]
    gm_w = wts["wo_m"].shape[0]
    d_ff = wts["w_up"].shape[1]
    row = lambda i: (i, 0)
    full2 = lambda i: (0, 0)
    once = dict(pipeline_mode=pl.Buffered(1))
    if a_transposed:
        per_seq = seq // tm
        a_spec = pl.BlockSpec((1, att_w, tm), lambda i: (i // per_seq, 0, i % per_seq))
    else:
        a_spec = pl.BlockSpec((tm, att_w), row)
    return pl.pallas_call(
        functools.partial(_mlp_kernel, a_transposed=a_transposed, ff_chunk=min(d_ff, 1024)),
        grid=(n // tm,),
        in_specs=[
            pl.BlockSpec((tm, d_model), row),
            a_spec,
            pl.BlockSpec((tm, gm_w), row),
            pl.BlockSpec((1, att_w), full2),
            pl.BlockSpec((att_w, d_model), full2, **once),
            pl.BlockSpec((gm_w, d_model), full2, **once),
            pl.BlockSpec((1, d_model), full2),
            pl.BlockSpec((d_model, d_ff), full2, **once),
            pl.BlockSpec((d_ff, d_model), full2, **once),
        ],
        out_specs=pl.BlockSpec((tm, d_model), row),
        out_shape=jax.ShapeDtypeStruct((n, d_model), _F32),
        compiler_params=pltpu.CompilerParams(dimension_semantics=("arbitrary",),
                                             vmem_limit_bytes=_VMEM_LIMIT),
        name="merge_mlp_prompt" if a_transposed else "merge_mlp_sample",
    )(x2d, a, mixn, wts["ag"], wts["wo_a"], wts["wo_m"], wts["ln2_g"], wts["w_up"], wts["w_down"])


def _topk_rows(gate, n_valid, k):
    nb = gate.shape[0]
    rows = lax.broadcasted_iota(jnp.int32, gate.shape, 0)
    valid = rows < n_valid
    g = jnp.where(valid, gate, -jnp.inf)
    sel = jnp.zeros(gate.shape, jnp.bool_)
    for _ in range(min(k, nb)):
        best = jnp.max(g, axis=0, keepdims=True)
        first = jnp.min(jnp.where(g == best, rows, nb), axis=0, keepdims=True)
        hit = rows == first
        sel = jnp.logical_or(sel, hit)
        g = jnp.where(hit, -jnp.inf, g)
    return jnp.logical_and(sel, valid)


def _moba_prompt_kernel(qt_ref, qa_ref, ka_ref, va_ref, ksum_ref, causal_ref, o_ref,
                        sel_ref, m_ref, alpha_ref, acc_ref, t_ref, p_ref):
    i = pl.program_id(2)
    tq = qt_ref.shape[3]
    nb = ka_ref.shape[2]
    qt = qt_ref[0, 0]
    row_head = lax.broadcasted_iota(jnp.int32, qt.shape, 0) // HEAD_DIM
    kmean = ksum_ref[0, 0] * (1.0 / MOBA_BLOCK)
    for r in range(2):
        gate = jnp.dot(kmean, jnp.where(row_head == r, qt, 0.0), precision=lax.Precision.HIGHEST,
                       preferred_element_type=_F32)
        sel_ref[r] = _topk_rows(gate, i, MOBA_TOPK).astype(_F32)
        m_ref[r] = jnp.full((1, tq), _NEG, _F32)
        acc_ref[r] = jnp.zeros((_V_ROWS, tq), _F32)
    alpha_ref[...] = jnp.ones(alpha_ref.shape, _F32)
    p_ref[...] = jnp.zeros(p_ref.shape, p_ref.dtype)

    def blocks(k):
        first = jnp.where(k == 0, i, jnp.minimum(2 * k - 1, nb - 1))
        return first, jnp.minimum(2 * k, nb - 1)

    def scores(k, slot):
        for r in range(2):
            for d, n in enumerate(blocks(k)):
                t_ref[slot, r, d] = _dot(ka_ref[0, r, n], qa_ref[0, r])

    def softmax(k, slot):
        own = k == 0
        n0, n1 = blocks(k)
        mask = causal_ref[own.astype(jnp.int32)]
        for r in range(2):
            t0 = t_ref[slot, r, 0] + mask
            t1 = t_ref[slot, r, 1]
            c0 = jnp.logical_or(sel_ref[r, pl.ds(n0, 1), :] > 0.5, own)
            c1 = sel_ref[r, pl.ds(n1, 1), :] > 0.5
            m = m_ref[r]
            m_new = jnp.maximum(m, jnp.maximum(
                jnp.where(c0, jnp.max(t0, axis=0, keepdims=True), _NEG),
                jnp.where(c1, jnp.max(t1, axis=0, keepdims=True), _NEG)))
            p_ref[slot, r, 0] = jnp.exp2(t0 - jnp.where(c0, m_new, -_NEG)).astype(_BF16)
            p_ref[slot, r, 1] = jnp.exp2(t1 - jnp.where(c1, m_new, -_NEG)).astype(_BF16)
            alpha_ref[slot, r] = jnp.exp2(m - m_new)
            m_ref[r] = m_new

    def weighted_values(k, slot):
        n0, n1 = blocks(k)
        for r in range(2):
            acc_ref[r] = (alpha_ref[slot, r] * acc_ref[r]
                          + _dot(va_ref[0, r, n0], p_ref[slot, r, 0])
                          + _dot(va_ref[0, r, n1], p_ref[slot, r, 1]))

    def step(kk, carry):
        k = 2 * kk
        weighted_values(jnp.maximum(k - 1, 0), 1)
        softmax(k, 0)
        scores(k + 1, 1)
        weighted_values(k, 0)
        scores(k + 2, 0)
        softmax(k + 1, 1)
        return carry

    n_items = lax.div(i + 2, 2)
    n_steps = lax.div(n_items + 1, 2)
    scores(jnp.int32(0), 0)
    lax.fori_loop(0, n_steps, step, 0)
    weighted_values(2 * n_steps - 1, 1)
    for r in range(2):
        acc = acc_ref[r]
        o_ref[0, 0, r * HEAD_DIM:(r + 1) * HEAD_DIM, :] = (
            acc[0:HEAD_DIM] / acc[HEAD_DIM:HEAD_DIM + 1])


def _moba_prompt(qt, qa, ka, va, ksum, causal):
    batch, n_pair, _, seq = qt.shape
    nb = ka.shape[2]
    tq = MOBA_BLOCK
    return pl.pallas_call(
        _moba_prompt_kernel,
        grid=(batch, n_pair, nb),
        in_specs=[
            pl.BlockSpec((1, 1, 128, tq), lambda b, p, i: (b, p, 0, i)),
            pl.BlockSpec((1, 2, 128, tq), lambda b, p, i: (b, p, 0, i)),
            pl.BlockSpec((1, 2, nb, MOBA_BLOCK, 128), lambda b, p, i: (b, p, 0, 0, 0)),
            pl.BlockSpec((1, 2, nb, _V_ROWS, MOBA_BLOCK), lambda b, p, i: (b, p, 0, 0, 0)),
            pl.BlockSpec((1, 1, nb, 128), lambda b, p, i: (b, p, 0, 0)),
            pl.BlockSpec((2, MOBA_BLOCK, tq), lambda b, p, i: (0, 0, 0)),
        ],
        out_specs=pl.BlockSpec((1, 1, 128, tq), lambda b, p, i: (b, p, 0, i)),
        out_shape=jax.ShapeDtypeStruct((batch, n_pair, 128, seq), _F32),
        scratch_shapes=[pltpu.VMEM((2, nb, tq), _F32),
                        pltpu.VMEM((2, 1, tq), _F32),
                        pltpu.VMEM((2, 2, 1, tq), _F32),
                        pltpu.VMEM((2, _V_ROWS, tq), _F32),
                        pltpu.VMEM((2, 2, 2, MOBA_BLOCK, tq), _F32),
                        pltpu.VMEM((2, 2, 2, MOBA_BLOCK, tq), _BF16)],
        compiler_params=pltpu.CompilerParams(
            dimension_semantics=("arbitrary", "arbitrary", "arbitrary"),
            vmem_limit_bytes=_VMEM_LIMIT),
        name="moba_prompt",
    )(qt, qa, ka, va, ksum, causal)


_RING = 32
_AHEAD = 24
_PAGE_GROUP = 8


def _moba_sample_kernel(pt_ref, q_ref, kn_ref, vn_ref, alibi_ref, newbias_ref,
                        ck_ref, cv_ref, o_ref, kring_ref, vring_ref, ksem_ref, vsem_ref,
                        s_ref, *, n_heads):
    b = pl.program_id(0)
    n_seq = pl.num_programs(0)
    n_pages = pt_ref.shape[1]
    tn, att_w = q_ref.shape
    q_rows = n_heads * tn
    ppb = MOBA_BLOCK // PAGE_SIZE
    nb = n_pages // ppb
    i0 = b * n_pages

    def copy(src_ref, ring_ref, sem_ref, page, slot):
        return pltpu.make_async_copy(src_ref.at[page], ring_ref.at[slot], sem_ref.at[slot])

    def fetch(src_ref, ring_ref, sem_ref, i):
        seq = jnp.minimum(i // n_pages, n_seq - 1)
        copy(src_ref, ring_ref, sem_ref, pt_ref[seq, i % n_pages], i % _RING).start()

    def wait(src_ref, ring_ref, sem_ref, i):
        copy(src_ref, ring_ref, sem_ref, 0, i % _RING).wait()

    k_stream = (ck_ref, kring_ref, ksem_ref)
    v_stream = (cv_ref, vring_ref, vsem_ref)

    @pl.when(b == 0)
    def _():
        for i in range(_AHEAD):
            fetch(*k_stream, jnp.int32(i))
            fetch(*v_stream, jnp.int32(i))

    q = q_ref[...]
    qrow = lax.broadcasted_iota(jnp.int32, (q_rows, att_w), 0)
    qlane = lax.broadcasted_iota(jnp.int32, (q_rows, att_w), 1)
    qbd = jnp.where(qrow // tn == qlane // HEAD_DIM, jnp.concatenate([q] * n_heads, axis=0), 0.0)
    qbd = (qbd * HEAD_DIM ** -0.5).astype(_BF16)
    lane = lax.broadcasted_iota(jnp.int32, (q_rows, 128), 1)

    def k_pages(step, gate):
        first = step * _PAGE_GROUP
        for u in range(_PAGE_GROUP):
            wait(*k_stream, i0 + first + u)
        for u in range(_PAGE_GROUP):
            slot = (i0 + first + u) % _RING
            s = _dot(qbd, kring_ref[slot].astype(_BF16))
            s_ref[first + u] = s
            gate = gate + jnp.where(lane == (first + u) // ppb,
                                    jnp.sum(s, axis=1, keepdims=True), 0.0)
        for u in range(_PAGE_GROUP):
            fetch(*k_stream, i0 + first + u + _AHEAD)
        return gate

    gate = lax.fori_loop(0, n_pages // _PAGE_GROUP, k_pages, jnp.zeros((q_rows, 128), _F32))

    gate = jnp.where(lane < nb, gate, -jnp.inf)
    sel = jnp.zeros(gate.shape, jnp.bool_)
    for _ in range(min(MOBA_TOPK, nb)):
        best = jnp.max(gate, axis=1, keepdims=True)
        first = jnp.min(jnp.where(gate == best, lane, 128), axis=1, keepdims=True)
        hit = lane == first
        sel = jnp.logical_or(sel, hit)
        gate = jnp.where(hit, -jnp.inf, gate)
    sel = jnp.where(jnp.logical_and(sel, lane < nb), 1.0, 0.0)

    slope = alibi_ref[0]
    origin = alibi_ref[1]

    def logits_pages(step, run_max):
        for u in range(_PAGE_GROUP):
            j = step * _PAGE_GROUP + u
            pos = (lane + j * PAGE_SIZE).astype(_F32)
            chosen = jnp.sum(jnp.where(lane == j // ppb, sel, 0.0), axis=1, keepdims=True) > 0.5
            logit = jnp.where(chosen, s_ref[j] + (slope * pos + origin), _NEG)
            s_ref[j] = logit
            run_max = jnp.maximum(run_max, logit)
        return run_max

    run_max = lax.fori_loop(0, n_pages // _PAGE_GROUP, logits_pages,
                            jnp.full((q_rows, 128), _NEG, _F32))
    logit_new = lax.dot_general(qbd, kn_ref[...].astype(_BF16), _NT,
                                preferred_element_type=_F32) + newbias_ref[...]
    m = jnp.maximum(jnp.max(run_max, axis=1, keepdims=True), jnp.max(logit_new, axis=1, keepdims=True))
    m_tile = jnp.broadcast_to(m, (q_rows, 128))

    def exp_pages(step, run_sum):
        for u in range(_PAGE_GROUP):
            j = step * _PAGE_GROUP + u
            p = jnp.exp(s_ref[j] - m_tile)
            s_ref[j] = p
            run_sum = run_sum + p
        return run_sum

    run_sum = lax.fori_loop(0, n_pages // _PAGE_GROUP, exp_pages, jnp.zeros((q_rows, 128), _F32))
    p_new = jnp.exp(logit_new - m)
    inv = 1.0 / (jnp.sum(run_sum, axis=1, keepdims=True) + jnp.sum(p_new, axis=1, keepdims=True))
    inv_tile = jnp.broadcast_to(inv, (q_rows, 128))

    def v_pages(step, acc):
        first = step * _PAGE_GROUP
        for u in range(_PAGE_GROUP):
            wait(*v_stream, i0 + first + u)
        for u in range(_PAGE_GROUP):
            slot = (i0 + first + u) % _RING
            p = (s_ref[first + u] * inv_tile).astype(_BF16)
            acc = acc + lax.dot_general(p, vring_ref[slot].astype(_BF16), _NT,
                                        preferred_element_type=_F32)
        for u in range(_PAGE_GROUP):
            fetch(*v_stream, i0 + first + u + _AHEAD)
        return acc

    acc = _dot((p_new * inv).astype(_BF16), vn_ref[...].astype(_BF16))
    acc = lax.fori_loop(0, n_pages // _PAGE_GROUP, v_pages, acc)

    @pl.when(b == n_seq - 1)
    def _():
        for i in range(_AHEAD):
            wait(*k_stream, n_seq * n_pages + i)
            wait(*v_stream, n_seq * n_pages + i)

    lane_head = lax.broadcasted_iota(jnp.int32, (tn, att_w), 1) // HEAD_DIM
    out = jnp.zeros((tn, att_w), _F32)
    for h in range(n_heads):
        out = out + jnp.where(lane_head == h, acc[h * tn:(h + 1) * tn, :], 0.0)
    o_ref[...] = out


def _sample_tables(slopes, tn, n_pages):
    n_heads = slopes.shape[0]
    n_keys = n_pages * PAGE_SIZE
    row = jnp.arange(n_heads * tn)
    slope_row = slopes[row // tn][:, None]
    tok = (row % tn).astype(_F32)[:, None]
    alibi = jnp.stack([jnp.broadcast_to(slope_row, (n_heads * tn, 128)),
                       jnp.broadcast_to(-slope_row * (float(n_keys) + tok), (n_heads * tn, 128))])
    dist = tok - jnp.arange(tn, dtype=_F32)[None, :]
    newbias = jnp.where(dist >= 0.0, -slope_row * dist, _NEG)
    return alibi, newbias


def _moba_sample(q, k_new, v_new, cache_k, cache_v, pages, tables, tn):
    n, att_w = q.shape
    n_heads = att_w // HEAD_DIM
    n_seq, n_pages = pages.shape
    assert n_pages % _PAGE_GROUP == 0 and _AHEAD + _PAGE_GROUP <= _RING
    row = lambda b, pt: (b, 0)
    return pl.pallas_call(
        functools.partial(_moba_sample_kernel, n_heads=n_heads),
        grid_spec=pltpu.PrefetchScalarGridSpec(
            num_scalar_prefetch=1,
            grid=(n_seq,),
            in_specs=[
                pl.BlockSpec((tn, att_w), row),
                pl.BlockSpec((tn, att_w), row),
                pl.BlockSpec((tn, att_w), row),
                *(pl.BlockSpec(t.shape, lambda b, pt, nd=t.ndim: (0,) * nd) for t in tables),
                pl.BlockSpec(memory_space=pl.ANY),
                pl.BlockSpec(memory_space=pl.ANY),
            ],
            out_specs=pl.BlockSpec((tn, att_w), row),
            scratch_shapes=[
                pltpu.VMEM((_RING, att_w, PAGE_SIZE), _F32),
                pltpu.VMEM((_RING, att_w, PAGE_SIZE), _F32),
                pltpu.SemaphoreType.DMA((_RING,)),
                pltpu.SemaphoreType.DMA((_RING,)),
                pltpu.VMEM((n_pages, n_heads * tn, PAGE_SIZE), _F32),
            ],
        ),
        out_shape=jax.ShapeDtypeStruct((n, att_w), _F32),
        compiler_params=pltpu.CompilerParams(dimension_semantics=("arbitrary",),
                                             vmem_limit_bytes=_VMEM_LIMIT),
        name="moba_sample",
    )(pages, q, k_new, v_new, *tables, cache_k, cache_v)


def _alibi_slopes(n_heads):
    h = jnp.arange(1, n_heads + 1, dtype=_F32)
    return jnp.exp2(-8.0 * h / n_heads)


def _split_terms(x):
    terms = []
    for _ in range(_POS_TERMS):
        bits = lax.bitcast_convert_type(x, jnp.uint32) & jnp.uint32(0xFFFF0000)
        t = lax.bitcast_convert_type(bits, _F32)
        terms.append(t.astype(_BF16))
        x = x - t
    return terms


def _position_tables(slopes, seq):
    n_heads = slopes.shape[0]
    pos = jnp.arange(seq, dtype=_F32)
    val = (slopes * _LOG2E)[:, None] * pos[None, :]
    start = HEAD_DIM * (1 - jnp.arange(n_heads) % 2)
    slot = jnp.arange(128)[None, :] - start[:, None]

    def table(terms, first, slot):
        out = jnp.where(jnp.logical_and(slot >= 0, slot < 2 * _POS_TERMS), 1.0, 0.0)
        for n, term in enumerate(terms):
            out = jnp.where(slot == first + n, term.astype(_F32), out)
        return out.astype(_BF16)

    kpos = table([t[:, :, None] for t in _split_terms(val)], 0, slot[:, None, :])
    qpos = table([t[:, None, :] for t in _split_terms(-val)], _POS_TERMS, slot[:, :, None])
    return kpos.reshape(n_heads, seq // MOBA_BLOCK, MOBA_BLOCK, 128), qpos


def kernel(x_prompt, x_sample, cache_k, cache_v, page_table, ln1_g, w_in, q_norm_g, k_norm_g,
           gm_ln_g, gm_ln_b, w_s, b_s, att_out_g, gm_out_g, w_out, ln2_g, w_up, w_down):
    batch, seq, d_model = x_prompt.shape
    dec_batch, tn, _ = x_sample.shape
    depth, n_pool, page_size, n_heads, head_dim = cache_k.shape
    gm_heads, gm_dim = gm_ln_g.shape[1:]
    att_w = n_heads * head_dim
    gm_w = gm_heads * gm_dim
    n_pages = page_table.shape[1]
    assert (head_dim, gm_dim, page_size, w_s.shape[-1]) == (HEAD_DIM, GM_HEAD_DIM, PAGE_SIZE, CHUNK)
    assert att_w == gm_w and att_w % 128 == 0 and n_heads * tn <= 128 and 128 % n_heads == 0
    assert seq % MOBA_BLOCK == 0 and CHUNK % tn == 0
    assert (n_pages * PAGE_SIZE) % MOBA_BLOCK == 0 and n_pages * PAGE_SIZE >= MOBA_BLOCK
    tm = min(_ROW_TILE, seq)
    tm_s = min(_ROW_TILE, dec_batch * tn)
    assert seq % tm == 0 and tm % MOBA_BLOCK == 0 and (dec_batch * tn) % tm_s == 0 and tm_s % CHUNK == 0
    n_pair = att_w // 128
    nb = seq // MOBA_BLOCK

    slopes = _alibi_slopes(n_heads)
    kpos, qpos = _position_tables(slopes, seq)
    group = jnp.arange(att_w) // HEAD_DIM
    gsum = (group[:, None] == group[None, :]).astype(_BF16)
    sample_tables = _sample_tables(slopes, tn, n_pages)
    rel = jnp.arange(MOBA_BLOCK)
    causal = jnp.stack([jnp.zeros((MOBA_BLOCK, MOBA_BLOCK), _F32),
                        jnp.where(rel[:, None] <= rel[None, :], 0.0, _NEG)])

    xp = x_prompt.reshape(batch * seq, d_model)
    xs = x_sample.reshape(dec_batch * tn, d_model)
    ck = cache_k.transpose(0, 1, 3, 4, 2).reshape(depth * n_pool, att_w, page_size)
    cv = cache_v.transpose(0, 1, 3, 4, 2).reshape(depth * n_pool, att_w, page_size)
    outs = [[] for _ in range(5)]
    for l in range(depth):
        wts = dict(
            ln1_g=ln1_g[l][None], w_in=w_in[l].astype(_BF16),
            qg=q_norm_g[l].reshape(1, att_w), kg=k_norm_g[l].reshape(1, att_w),
            glg=gm_ln_g[l].reshape(1, gm_w), glb=gm_ln_b[l].reshape(1, gm_w),
            gsum=gsum, gmo=gm_out_g[l][None], ag=att_out_g[l][None],
            wo_a=w_out[l][:att_w].astype(_BF16), wo_m=w_out[l][att_w:].astype(_BF16),
            ln2_g=ln2_g[l][None], w_up=w_up[l].astype(_BF16), w_down=w_down[l].astype(_BF16))
        w_tril = jnp.tril(w_s[l])
        wmix_p = w_tril.transpose(1, 0, 2).reshape(CHUNK, gm_heads * CHUNK).astype(_BF16)
        bmix_p = jnp.repeat(b_s[l].T, GM_HEAD_DIM, axis=1)
        eye = jnp.eye(CHUNK // tn, dtype=_F32)
        w_blk = jnp.einsum("ab,hij->haibj", eye, w_tril[:, :tn, :tn]).reshape(gm_heads, CHUNK, CHUNK)
        wmix_s = w_blk.transpose(1, 0, 2).reshape(CHUNK, gm_heads * CHUNK).astype(_BF16)
        bmix_s = jnp.tile(jnp.repeat(b_s[l][:, :tn].T, GM_HEAD_DIM, axis=1), (CHUNK // tn, 1))

        k_p, v_p, qt, qa, ka, va, ksum, mixn_p = _project(xp, seq, wts, wmix_p, bmix_p, True, tm,
                                                          (kpos, qpos))
        ksum = ksum.reshape(batch, nb, n_pair, 128).transpose(0, 2, 1, 3)
        a_t = _moba_prompt(qt, qa, ka, va, ksum, causal)
        xp = _merge_mlp(xp, a_t.reshape(batch, att_w, seq), mixn_p, wts, seq, tm, True)

        k_s, v_s, q_s, g_s, mixn_s = _project(xs, tn, wts, wmix_s, bmix_s, False, tm_s)
        a_s = _moba_sample(q_s, k_s, v_s, ck, cv, page_table + l * n_pool, sample_tables, tn)
        xs = _merge_mlp(xs, a_s, mixn_s, wts, tn, tm_s, False)

        outs[0].append(k_p.reshape(batch, n_heads, head_dim, seq).transpose(0, 3, 1, 2))
        outs[1].append(v_p.reshape(batch, n_heads, head_dim, seq).transpose(0, 3, 1, 2))
        outs[2].append(k_s.reshape(dec_batch, tn, n_heads, head_dim))
        outs[3].append(v_s.reshape(dec_batch, tn, n_heads, head_dim))
        outs[4].append(g_s.reshape(dec_batch, tn, gm_heads, gm_dim))
    return (xp.reshape(batch, seq, d_model), xs.reshape(dec_batch, tn, d_model),
            *(jnp.stack(o) for o in outs))
```

```python
import functools

import jax
import jax.numpy as jnp
from jax import lax
from jax.experimental import pallas as pl
from jax.experimental.pallas import tpu as pltpu

HEAD_DIM = 64
GM_HEAD_DIM = 64
MOBA_BLOCK = 256
MOBA_TOPK = 3
CHUNK = 128
PAGE_SIZE = 128
NORM_EPS = 1e-6

_F32 = jnp.float32
_BF16 = jnp.bfloat16
_NEG = -1e30
_VMEM_LIMIT = 56 * 1024 * 1024
_ROW_TILE = 512
_LOG2E = 1.4426950408889634
_V_ROWS = 80
_POS_TERMS = 3

_NT = (((1,), (1,)), ((), ()))
_TN = (((0,), (0,)), ((), ()))


def _dot(a, b):
    return jnp.dot(a, b, preferred_element_type=_F32)


def _split_bf16(x):
    hi = x.astype(_BF16)
    lo = (x - hi.astype(_F32)).astype(_BF16)
    return hi, lo


def _group_mean(x, gsum, width):
    hi, lo = _split_bf16(x)
    return (_dot(hi, gsum) + _dot(lo, gsum)) * (1.0 / width)


def _rms_rows(x, g):
    return x * lax.rsqrt(jnp.mean(x * x, axis=-1, keepdims=True) + NORM_EPS) * g


def _proj_kernel(x_ref, ln_g_ref, w_ref, qg_ref, kg_ref, glg_ref, glb_ref, gsum_ref,
                 wmix_ref, bmix_ref, gmo_ref, *refs, att_w, gm_w, prompt):
    if prompt:
        (kpos_ref, qpos_ref,
         k_ref, v_ref, qt_ref, qa_ref, ka_ref, va_ref, ksum_ref, mixn_ref) = refs
    else:
        k_ref, v_ref, q_ref, g_ref, mixn_ref = refs
    tm = x_ref.shape[0]
    gsum = gsum_ref[...]

    xn = _rms_rows(x_ref[...], ln_g_ref[...]).astype(_BF16)

    def zcols(lo, width):
        return _dot(xn, w_ref[:, lo:lo + width])

    def head_rms(z, g):
        return z * lax.rsqrt(_group_mean(z * z, gsum, HEAD_DIM) + NORM_EPS) * g

    q = head_rms(zcols(0, att_w), qg_ref[...])
    k = head_rms(zcols(att_w, att_w), kg_ref[...])
    v = zcols(2 * att_w, att_w)
    if prompt:
        vt = v.T
        k_ref[0] = k.T
        v_ref[0] = vt
        n_pair = att_w // 128
        qt = q.T
        qtb = (qt * (HEAD_DIM ** -0.5 * _LOG2E)).astype(_BF16)
        kb = k.astype(_BF16)
        vtb = vt.astype(_BF16)
        lane_half = lax.broadcasted_iota(jnp.int32, (MOBA_BLOCK, 128), 1) // HEAD_DIM
        row_half = lax.broadcasted_iota(jnp.int32, (128, tm), 0) // HEAD_DIM
        ones_row = (lax.broadcasted_iota(jnp.int32, (_V_ROWS - HEAD_DIM, MOBA_BLOCK), 0) == 0
                    ).astype(_BF16)
        for h in range(att_w // HEAD_DIM):
            pair = slice((h // 2) * 128, (h // 2 + 1) * 128)
            qa = jnp.where(row_half == h % 2, qtb[pair], qpos_ref[h])
            for j in range(tm // MOBA_BLOCK):
                blk = slice(j * MOBA_BLOCK, (j + 1) * MOBA_BLOCK)
                qa_ref[0, h, j] = qa[:, blk]
                ka_ref[0, h, j] = jnp.where(lane_half == h % 2, kb[blk, pair], kpos_ref[h, j])
                va_ref[0, h, j, 0:HEAD_DIM, :] = vtb[h * HEAD_DIM:(h + 1) * HEAD_DIM, blk]
                va_ref[0, h, j, HEAD_DIM:_V_ROWS, :] = ones_row
        for j in range(tm // MOBA_BLOCK):
            blk = slice(j * MOBA_BLOCK, (j + 1) * MOBA_BLOCK)
            qt_ref[0, :, j] = qt[:, blk].reshape(n_pair, 128, MOBA_BLOCK)
            ksum_ref[0, j:j + 1, :] = jnp.sum(k[blk], axis=0, keepdims=True)
    else:
        k_ref[...] = k
        v_ref[...] = v
        q_ref[...] = q

    u = jax.nn.gelu(zcols(3 * att_w, gm_w))
    gg = jax.nn.gelu(zcols(3 * att_w + gm_w, gm_w))
    mu = _group_mean(gg, gsum, GM_HEAD_DIM)
    cen = gg - mu
    var = _group_mean(cen * cen, gsum, GM_HEAD_DIM)
    g = cen * lax.rsqrt(var + NORM_EPS) * glg_ref[...] + glb_ref[...]
    if not prompt:
        g_ref[...] = g

    n_grp = gm_w // GM_HEAD_DIM
    gb = g.astype(_BF16)
    wmix = wmix_ref[...]
    bmix = bmix_ref[...]
    lane_grp = lax.broadcasted_iota(jnp.int32, (CHUNK, gm_w), 1) // GM_HEAD_DIM
    pieces = []
    for c in range(tm // CHUNK):
        rows = slice(c * CHUNK, (c + 1) * CHUNK)
        stack = jnp.concatenate(
            [jnp.where(lane_grp == h, gb[rows], jnp.zeros_like(gb[rows])) for h in range(n_grp)],
            axis=0)
        mixed = _dot(wmix, stack) + bmix
        pieces.append(u[rows] * mixed)
    m = jnp.concatenate(pieces, axis=0)
    mixn_ref[...] = _rms_rows(m, gmo_ref[...]).astype(_BF16)


def _project(x2d, seq, wts, wmix, bmix, prompt, tm, tables=()):
    n, d_model = x2d.shape
    att_w = wts["qg"].shape[1]
    gm_w = wts["glg"].shape[1]
    steps = n // tm
    per_seq = seq // tm if prompt else 1
    row = lambda i: (i, 0)
    full2 = lambda i: (0, 0)
    in_specs = [
        pl.BlockSpec((tm, d_model), row),
        pl.BlockSpec((1, d_model), full2),
        pl.BlockSpec(wts["w_in"].shape, full2),
        pl.BlockSpec((1, att_w), full2),
        pl.BlockSpec((1, att_w), full2),
        pl.BlockSpec((1, gm_w), full2),
        pl.BlockSpec((1, gm_w), full2),
        pl.BlockSpec(wts["gsum"].shape, full2),
        pl.BlockSpec(wmix.shape, full2),
        pl.BlockSpec(bmix.shape, full2),
        pl.BlockSpec((1, gm_w), full2),
    ]
    f32_rows = jax.ShapeDtypeStruct((n, att_w), _F32)
    if prompt:
        batch = n // seq
        n_pair = att_w // 128
        n_heads = att_w // HEAD_DIM
        blk_per_tile = tm // MOBA_BLOCK
        nb = seq // MOBA_BLOCK
        in_specs += [
            pl.BlockSpec((n_heads, blk_per_tile, MOBA_BLOCK, 128), lambda i: (0, i % per_seq, 0, 0)),
            pl.BlockSpec((n_heads, 128, tm), lambda i: (0, 0, i % per_seq)),
        ]
        blocks = lambda i: (i // per_seq, 0, i % per_seq, 0, 0)
        kv_t = jax.ShapeDtypeStruct((batch, att_w, seq), _F32)
        kv_t_spec = pl.BlockSpec((1, att_w, tm), lambda i: (i // per_seq, 0, i % per_seq))
        out_shape = (
            kv_t, kv_t,
            jax.ShapeDtypeStruct((batch, n_pair, nb, 128, MOBA_BLOCK), _F32),
            jax.ShapeDtypeStruct((batch, n_heads, nb, 128, MOBA_BLOCK), _BF16),
            jax.ShapeDtypeStruct((batch, n_heads, nb, MOBA_BLOCK, 128), _BF16),
            jax.ShapeDtypeStruct((batch, n_heads, nb, _V_ROWS, MOBA_BLOCK), _BF16),
            jax.ShapeDtypeStruct((steps, blk_per_tile, att_w), _F32),
            jax.ShapeDtypeStruct((n, gm_w), _BF16),
        )
        out_specs = (
            kv_t_spec, kv_t_spec,
            pl.BlockSpec((1, n_pair, blk_per_tile, 128, MOBA_BLOCK), blocks),
            pl.BlockSpec((1, n_heads, blk_per_tile, 128, MOBA_BLOCK), blocks),
            pl.BlockSpec((1, n_heads, blk_per_tile, MOBA_BLOCK, 128), blocks),
            pl.BlockSpec((1, n_heads, blk_per_tile, _V_ROWS, MOBA_BLOCK), blocks),
            pl.BlockSpec((1, blk_per_tile, att_w), lambda i: (i, 0, 0)),
            pl.BlockSpec((tm, gm_w), row),
        )
    else:
        out_shape = (f32_rows, f32_rows, f32_rows,
                     jax.ShapeDtypeStruct((n, gm_w), _F32),
                     jax.ShapeDtypeStruct((n, gm_w), _BF16))
        out_specs = tuple(pl.BlockSpec((tm, att_w), row) for _ in range(5))
    return pl.pallas_call(
        functools.partial(_proj_kernel, att_w=att_w, gm_w=gm_w, prompt=prompt),
        grid=(steps,),
        in_specs=in_specs,
        out_specs=out_specs,
        out_shape=out_shape,
        compiler_params=pltpu.CompilerParams(dimension_semantics=("arbitrary",),
                                             vmem_limit_bytes=_VMEM_LIMIT),
        name="project_prompt" if prompt else "project_sample",
    )(x2d, wts["ln1_g"], wts["w_in"], wts["qg"], wts["kg"], wts["glg"], wts["glb"], wts["gsum"],
      wmix, bmix, wts["gmo"], *tables)


def _mlp_kernel(x_ref, a_ref, mixn_ref, ag_ref, wo_a_ref, wo_m_ref, ln2_ref, wup_ref, wdn_ref,
                o_ref, *, a_transposed, ff_chunk):
    if a_transposed:
        att_w = ag_ref.shape[1]
        a = jnp.concatenate([a_ref[0, :, j].reshape(att_w, MOBA_BLOCK).T
                             for j in range(a_ref.shape[2])], axis=0)
    else:
        a = a_ref[...]
    an = _rms_rows(a, ag_ref[...]).astype(_BF16)
    x1 = x_ref[...] + _dot(an, wo_a_ref[...]) + _dot(mixn_ref[...], wo_m_ref[...])
    hn = _rms_rows(x1, ln2_ref[...]).astype(_BF16)
    y = None
    for c in range(wup_ref.shape[1] // ff_chunk):
        cols = slice(c * ff_chunk, (c + 1) * ff_chunk)
        h = jnp.square(jnp.maximum(_dot(hn, wup_ref[:, cols]), 0.0)).astype(_BF16)
        d = _dot(h, wdn_ref[cols, :])
        y = d if y is None else y + d
    o_ref[...] = x1 + y


def _merge_mlp(x2d, a, mixn, wts, seq, tm, a_transposed):
    n, d_model = x2d.shape
    att_w = wts["wo_a"].shape[0]
    gm_w = wts["wo_m"].shape[0]
    d_ff = wts["w_up"].shape[1]
    row = lambda i: (i, 0)
    full2 = lambda i: (0, 0)
    once = dict(pipeline_mode=pl.Buffered(1))
    if a_transposed:
        per_seq = seq // tm
        a_spec = pl.BlockSpec((1, att_w // 128, tm // MOBA_BLOCK, 128, MOBA_BLOCK),
                              lambda i: (i // per_seq, 0, i % per_seq, 0, 0))
    else:
        a_spec = pl.BlockSpec((tm, att_w), row)
    return pl.pallas_call(
        functools.partial(_mlp_kernel, a_transposed=a_transposed, ff_chunk=min(d_ff, 1024)),
        grid=(n // tm,),
        in_specs=[
            pl.BlockSpec((tm, d_model), row),
            a_spec,
            pl.BlockSpec((tm, gm_w), row),
            pl.BlockSpec((1, att_w), full2),
            pl.BlockSpec((att_w, d_model), full2, **once),
            pl.BlockSpec((gm_w, d_model), full2, **once),
            pl.BlockSpec((1, d_model), full2),
            pl.BlockSpec((d_model, d_ff), full2, **once),
            pl.BlockSpec((d_ff, d_model), full2, **once),
        ],
        out_specs=pl.BlockSpec((tm, d_model), row),
        out_shape=jax.ShapeDtypeStruct((n, d_model), _F32),
        compiler_params=pltpu.CompilerParams(dimension_semantics=("arbitrary",),
                                             vmem_limit_bytes=_VMEM_LIMIT),
        name="merge_mlp_prompt" if a_transposed else "merge_mlp_sample",
    )(x2d, a, mixn, wts["ag"], wts["wo_a"], wts["wo_m"], wts["ln2_g"], wts["w_up"], wts["w_down"])


def _topk_rows(gate, n_valid, k):
    nb = gate.shape[0]
    rows = lax.broadcasted_iota(jnp.int32, gate.shape, 0)
    valid = rows < n_valid
    g = jnp.where(valid, gate, -jnp.inf)
    sel = jnp.zeros(gate.shape, jnp.bool_)
    for _ in range(min(k, nb)):
        best = jnp.max(g, axis=0, keepdims=True)
        first = jnp.min(jnp.where(g == best, rows, nb), axis=0, keepdims=True)
        hit = rows == first
        sel = jnp.logical_or(sel, hit)
        g = jnp.where(hit, -jnp.inf, g)
    return jnp.logical_and(sel, valid)


def _moba_prompt_kernel(qt_ref, qa_ref, ka_ref, va_ref, ksum_ref, causal_ref, o_ref,
                        sel_ref, m_ref, alpha_ref, acc_ref, t_ref, p_ref):
    nb = ka_ref.shape[2]
    tq = MOBA_BLOCK
    n_total = sum((i + 2) // 2 for i in range(nb))

    kmean = ksum_ref[0, 0] * (1.0 / MOBA_BLOCK)
    lane_head = lax.broadcasted_iota(jnp.int32, kmean.shape, 1) // HEAD_DIM
    kmean2 = jnp.concatenate([jnp.where(lane_head == r, kmean, 0.0) for r in range(2)], axis=0)

    def select(i, carry):
        gate = jnp.dot(kmean2, qt_ref[0, 0, i], precision=lax.Precision.HIGHEST,
                       preferred_element_type=_F32)
        for r in range(2):
            sel_ref[i, r] = _topk_rows(gate[r * nb:(r + 1) * nb], i, MOBA_TOPK).astype(_F32)
        return carry

    lax.fori_loop(0, nb, select, 0)

    ones_row = jnp.where(lax.broadcasted_iota(jnp.int32, (_V_ROWS, tq), 0) == HEAD_DIM, 1.0, 0.0)
    for r in range(2):
        m_ref[r] = jnp.full((1, tq), _NEG, _F32)
        acc_ref[r] = ones_row
    alpha_ref[...] = jnp.ones(alpha_ref.shape, _F32)
    p_ref[1] = jnp.zeros(p_ref.shape[1:], p_ref.dtype)

    def advance(i, k):
        last = k + 1 >= lax.div(i + 2, 2)
        return jnp.where(last, i + 1, i), jnp.where(last, 0, k + 1)

    def blocks(i, k):
        tile = jnp.minimum(i, nb - 1)
        first = jnp.where(k == 0, tile, jnp.minimum(2 * k - 1, nb - 1))
        return tile, first, jnp.minimum(2 * k, nb - 1)

    def scores(item, slot):
        tile, *ns = blocks(*item)
        for r in range(2):
            for d, n in enumerate(ns):
                t_ref[slot, r, d] = _dot(ka_ref[0, r, n], qa_ref[0, r, tile])

    def softmax(item, slot):
        i, k = item
        tile, n0, n1 = blocks(i, k)
        real = i < nb
        own = jnp.logical_and(k == 0, real)
        mask = causal_ref[own.astype(jnp.int32)]
        for r in range(2):
            t0 = t_ref[slot, r, 0] + mask
            t1 = t_ref[slot, r, 1]
            c0 = jnp.logical_or(jnp.logical_and(sel_ref[tile, r, pl.ds(n0, 1), :] > 0.5, real), own)
            c1 = jnp.logical_and(sel_ref[tile, r, pl.ds(n1, 1), :] > 0.5, real)
            m = jnp.where(k == 0, _NEG, m_ref[r])
            m_new = jnp.maximum(m, jnp.maximum(
                jnp.where(c0, jnp.max(t0, axis=0, keepdims=True), _NEG),
                jnp.where(c1, jnp.max(t1, axis=0, keepdims=True), _NEG)))
            p_ref[slot, r, 0] = jnp.exp2(t0 - jnp.where(c0, m_new, -_NEG)).astype(_BF16)
            p_ref[slot, r, 1] = jnp.exp2(t1 - jnp.where(c1, m_new, -_NEG)).astype(_BF16)
            alpha_ref[slot, r] = jnp.where(real, jnp.exp2(m - m_new), 1.0)
            m_ref[r] = m_new

    def weighted_values(item, slot):
        tile, n0, n1 = blocks(*item)
        for r in range(2):
            acc = (alpha_ref[slot, r] * acc_ref[r]
                   + _dot(va_ref[0, r, n0], p_ref[slot, r, 0])
                   + _dot(va_ref[0, r, n1], p_ref[slot, r, 1]))
            acc_ref[r] = acc
            o_ref[0, 0, tile, r * HEAD_DIM:(r + 1) * HEAD_DIM, :] = (
                acc[0:HEAD_DIM] / acc[HEAD_DIM:HEAD_DIM + 1])

    def step(_, carry):
        before, item = carry
        after = advance(*item)
        after2 = advance(*after)
        weighted_values(before, 1)
        softmax(item, 0)
        scores(after, 1)
        weighted_values(item, 0)
        scores(after2, 0)
        softmax(after, 1)
        return after, after2

    first = (jnp.int32(0), jnp.int32(0))
    nothing = (jnp.int32(nb), jnp.int32(1))
    scores(first, 0)
    last, _ = lax.fori_loop(0, (n_total + 1) // 2, step, (nothing, first))
    weighted_values(last, 1)


def _moba_prompt(qt, qa, ka, va, ksum, causal):
    batch, n_pair, nb = qt.shape[:3]
    tq = MOBA_BLOCK
    per_pair = lambda b, p: (b, p, 0, 0, 0)
    return pl.pallas_call(
        _moba_prompt_kernel,
        grid=(batch, n_pair),
        in_specs=[
            pl.BlockSpec((1, 1, nb, 128, tq), per_pair),
            pl.BlockSpec((1, 2, nb, 128, tq), per_pair),
            pl.BlockSpec((1, 2, nb, MOBA_BLOCK, 128), per_pair),
            pl.BlockSpec((1, 2, nb, _V_ROWS, MOBA_BLOCK), per_pair),
            pl.BlockSpec((1, 1, nb, 128), lambda b, p: (b, p, 0, 0)),
            pl.BlockSpec((2, MOBA_BLOCK, tq), lambda b, p: (0, 0, 0)),
        ],
        out_specs=pl.BlockSpec((1, 1, nb, 128, tq), per_pair),
        out_shape=jax.ShapeDtypeStruct((batch, n_pair, nb, 128, tq), _F32),
        scratch_shapes=[pltpu.VMEM((nb, 2, nb, tq), _F32),
                        pltpu.VMEM((2, 1, tq), _F32),
                        pltpu.VMEM((2, 2, 1, tq), _F32),
                        pltpu.VMEM((2, _V_ROWS, tq), _F32),
                        pltpu.VMEM((2, 2, 2, MOBA_BLOCK, tq), _F32),
                        pltpu.VMEM((2, 2, 2, MOBA_BLOCK, tq), _BF16)],
        compiler_params=pltpu.CompilerParams(
            dimension_semantics=("arbitrary", "arbitrary"),
            vmem_limit_bytes=_VMEM_LIMIT),
        name="moba_prompt",
    )(qt, qa, ka, va, ksum, causal)


_RING = 64
_AHEAD = 56
_PAGE_GROUP = 8


def _moba_sample_kernel(pt_ref, q_ref, kn_ref, vn_ref, alibi_ref, newbias_ref,
                        ck_ref, cv_ref, o_ref, kring_ref, vring_ref, ksem_ref, vsem_ref,
                        s_ref, *, n_heads):
    b = pl.program_id(0)
    n_seq = pl.num_programs(0)
    n_pages = pt_ref.shape[1]
    tn, att_w = q_ref.shape
    q_rows = n_heads * tn
    ppb = MOBA_BLOCK // PAGE_SIZE
    nb = n_pages // ppb
    i0 = b * n_pages

    def copy(src_ref, ring_ref, sem_ref, page, slot):
        return pltpu.make_async_copy(src_ref.at[page], ring_ref.at[slot], sem_ref.at[slot])

    def fetch(src_ref, ring_ref, sem_ref, i):
        seq = jnp.minimum(i // n_pages, n_seq - 1)
        copy(src_ref, ring_ref, sem_ref, pt_ref[seq, i % n_pages], i % _RING).start()

    def wait(src_ref, ring_ref, sem_ref, i):
        copy(src_ref, ring_ref, sem_ref, 0, i % _RING).wait()

    k_stream = (ck_ref, kring_ref, ksem_ref)
    v_stream = (cv_ref, vring_ref, vsem_ref)

    @pl.when(b == 0)
    def _():
        for i in range(_AHEAD):
            fetch(*k_stream, jnp.int32(i))
            fetch(*v_stream, jnp.int32(i))

    q = q_ref[...]
    qrow = lax.broadcasted_iota(jnp.int32, (q_rows, att_w), 0)
    qlane = lax.broadcasted_iota(jnp.int32, (q_rows, att_w), 1)
    qbd = jnp.where(qrow // tn == qlane // HEAD_DIM, jnp.concatenate([q] * n_heads, axis=0), 0.0)
    qbd = (qbd * HEAD_DIM ** -0.5).astype(_BF16)
    lane = lax.broadcasted_iota(jnp.int32, (q_rows, 128), 1)

    def k_pages(step, gate):
        first = step * _PAGE_GROUP
        for u in range(_PAGE_GROUP):
            wait(*k_stream, i0 + first + u)
        for u in range(_PAGE_GROUP):
            slot = (i0 + first + u) % _RING
            s = _dot(qbd, kring_ref[slot].astype(_BF16))
            s_ref[first + u] = s
            gate = gate + jnp.where(lane == (first + u) // ppb,
                                    jnp.sum(s, axis=1, keepdims=True), 0.0)
        for u in range(_PAGE_GROUP):
            fetch(*k_stream, i0 + first + u + _AHEAD)
        return gate

    gate = lax.fori_loop(0, n_pages // _PAGE_GROUP, k_pages, jnp.zeros((q_rows, 128), _F32))

    gate = jnp.where(lane < nb, gate, -jnp.inf)
    sel = jnp.zeros(gate.shape, jnp.bool_)
    for _ in range(min(MOBA_TOPK, nb)):
        best = jnp.max(gate, axis=1, keepdims=True)
        first = jnp.min(jnp.where(gate == best, lane, 128), axis=1, keepdims=True)
        hit = lane == first
        sel = jnp.logical_or(sel, hit)
        gate = jnp.where(hit, -jnp.inf, gate)
    sel = jnp.where(jnp.logical_and(sel, lane < nb), 1.0, 0.0)

    slope = alibi_ref[0]
    origin = alibi_ref[1]

    def logits_pages(step, run_max):
        for u in range(_PAGE_GROUP):
            j = step * _PAGE_GROUP + u
            pos = (lane + j * PAGE_SIZE).astype(_F32)
            chosen = jnp.sum(jnp.where(lane == j // ppb, sel, 0.0), axis=1, keepdims=True) > 0.5
            logit = jnp.where(chosen, s_ref[j] + (slope * pos + origin), _NEG)
            s_ref[j] = logit
            run_max = jnp.maximum(run_max, logit)
        return run_max

    run_max = lax.fori_loop(0, n_pages // _PAGE_GROUP, logits_pages,
                            jnp.full((q_rows, 128), _NEG, _F32))
    logit_new = lax.dot_general(qbd, kn_ref[...].astype(_BF16), _NT,
                                preferred_element_type=_F32) + newbias_ref[...]
    m = jnp.maximum(jnp.max(run_max, axis=1, keepdims=True), jnp.max(logit_new, axis=1, keepdims=True))
    m_tile = jnp.broadcast_to(m, (q_rows, 128))

    def exp_pages(step, run_sum):
        for u in range(_PAGE_GROUP):
            j = step * _PAGE_GROUP + u
            p = jnp.exp(s_ref[j] - m_tile)
            s_ref[j] = p
            run_sum = run_sum + p
        return run_sum

    run_sum = lax.fori_loop(0, n_pages // _PAGE_GROUP, exp_pages, jnp.zeros((q_rows, 128), _F32))
    p_new = jnp.exp(logit_new - m)
    inv = 1.0 / (jnp.sum(run_sum, axis=1, keepdims=True) + jnp.sum(p_new, axis=1, keepdims=True))
    inv_tile = jnp.broadcast_to(inv, (q_rows, 128))

    def v_pages(step, acc):
        first = step * _PAGE_GROUP
        for u in range(_PAGE_GROUP):
            wait(*v_stream, i0 + first + u)
        for u in range(_PAGE_GROUP):
            slot = (i0 + first + u) % _RING
            p = (s_ref[first + u] * inv_tile).astype(_BF16)
            acc = acc + lax.dot_general(p, vring_ref[slot].astype(_BF16), _NT,
                                        preferred_element_type=_F32)
        for u in range(_PAGE_GROUP):
            fetch(*v_stream, i0 + first + u + _AHEAD)
        return acc

    acc = _dot((p_new * inv).astype(_BF16), vn_ref[...].astype(_BF16))
    acc = lax.fori_loop(0, n_pages // _PAGE_GROUP, v_pages, acc)

    @pl.when(b == n_seq - 1)
    def _():
        for i in range(_AHEAD):
            wait(*k_stream, n_seq * n_pages + i)
            wait(*v_stream, n_seq * n_pages + i)

    lane_head = lax.broadcasted_iota(jnp.int32, (tn, att_w), 1) // HEAD_DIM
    out = jnp.zeros((tn, att_w), _F32)
    for h in range(n_heads):
        out = out + jnp.where(lane_head == h, acc[h * tn:(h + 1) * tn, :], 0.0)
    o_ref[...] = out


def _sample_tables(slopes, tn, n_pages):
    n_heads = slopes.shape[0]
    n_keys = n_pages * PAGE_SIZE
    row = jnp.arange(n_heads * tn)
    slope_row = slopes[row // tn][:, None]
    tok = (row % tn).astype(_F32)[:, None]
    alibi = jnp.stack([jnp.broadcast_to(slope_row, (n_heads * tn, 128)),
                       jnp.broadcast_to(-slope_row * (float(n_keys) + tok), (n_heads * tn, 128))])
    dist = tok - jnp.arange(tn, dtype=_F32)[None, :]
    newbias = jnp.where(dist >= 0.0, -slope_row * dist, _NEG)
    return alibi, newbias


def _moba_sample(q, k_new, v_new, cache_k, cache_v, pages, tables, tn):
    n, att_w = q.shape
    n_heads = att_w // HEAD_DIM
    n_seq, n_pages = pages.shape
    assert n_pages % _PAGE_GROUP == 0 and _AHEAD + _PAGE_GROUP <= _RING
    row = lambda b, pt: (b, 0)
    return pl.pallas_call(
        functools.partial(_moba_sample_kernel, n_heads=n_heads),
        grid_spec=pltpu.PrefetchScalarGridSpec(
            num_scalar_prefetch=1,
            grid=(n_seq,),
            in_specs=[
                pl.BlockSpec((tn, att_w), row),
                pl.BlockSpec((tn, att_w), row),
                pl.BlockSpec((tn, att_w), row),
                *(pl.BlockSpec(t.shape, lambda b, pt, nd=t.ndim: (0,) * nd) for t in tables),
                pl.BlockSpec(memory_space=pl.ANY),
                pl.BlockSpec(memory_space=pl.ANY),
            ],
            out_specs=pl.BlockSpec((tn, att_w), row),
            scratch_shapes=[
                pltpu.VMEM((_RING, att_w, PAGE_SIZE), _F32),
                pltpu.VMEM((_RING, att_w, PAGE_SIZE), _F32),
                pltpu.SemaphoreType.DMA((_RING,)),
                pltpu.SemaphoreType.DMA((_RING,)),
                pltpu.VMEM((n_pages, n_heads * tn, PAGE_SIZE), _F32),
            ],
        ),
        out_shape=jax.ShapeDtypeStruct((n, att_w), _F32),
        compiler_params=pltpu.CompilerParams(dimension_semantics=("arbitrary",),
                                             vmem_limit_bytes=_VMEM_LIMIT),
        name="moba_sample",
    )(pages, q, k_new, v_new, *tables, cache_k, cache_v)


def _alibi_slopes(n_heads):
    h = jnp.arange(1, n_heads + 1, dtype=_F32)
    return jnp.exp2(-8.0 * h / n_heads)


def _split_terms(x):
    terms = []
    for _ in range(_POS_TERMS):
        bits = lax.bitcast_convert_type(x, jnp.uint32) & jnp.uint32(0xFFFF0000)
        t = lax.bitcast_convert_type(bits, _F32)
        terms.append(t.astype(_BF16))
        x = x - t
    return terms


def _position_tables(slopes, seq):
    n_heads = slopes.shape[0]
    pos = jnp.arange(seq, dtype=_F32)
    val = (slopes * _LOG2E)[:, None] * pos[None, :]
    start = HEAD_DIM * (1 - jnp.arange(n_heads) % 2)
    slot = jnp.arange(128)[None, :] - start[:, None]

    def table(terms, first, slot):
        out = jnp.where(jnp.logical_and(slot >= 0, slot < 2 * _POS_TERMS), 1.0, 0.0)
        for n, term in enumerate(terms):
            out = jnp.where(slot == first + n, term.astype(_F32), out)
        return out.astype(_BF16)

    kpos = table([t[:, :, None] for t in _split_terms(val)], 0, slot[:, None, :])
    qpos = table([t[:, None, :] for t in _split_terms(-val)], _POS_TERMS, slot[:, :, None])
    return kpos.reshape(n_heads, seq // MOBA_BLOCK, MOBA_BLOCK, 128), qpos


def kernel(x_prompt, x_sample, cache_k, cache_v, page_table, ln1_g, w_in, q_norm_g, k_norm_g,
           gm_ln_g, gm_ln_b, w_s, b_s, att_out_g, gm_out_g, w_out, ln2_g, w_up, w_down):
    batch, seq, d_model = x_prompt.shape
    dec_batch, tn, _ = x_sample.shape
    depth, n_pool, page_size, n_heads, head_dim = cache_k.shape
    gm_heads, gm_dim = gm_ln_g.shape[1:]
    att_w = n_heads * head_dim
    gm_w = gm_heads * gm_dim
    n_pages = page_table.shape[1]
    assert (head_dim, gm_dim, page_size, w_s.shape[-1]) == (HEAD_DIM, GM_HEAD_DIM, PAGE_SIZE, CHUNK)
    assert att_w == gm_w and att_w % 128 == 0 and n_heads * tn <= 128 and 128 % n_heads == 0
    assert seq % MOBA_BLOCK == 0 and CHUNK % tn == 0
    assert (n_pages * PAGE_SIZE) % MOBA_BLOCK == 0 and n_pages * PAGE_SIZE >= MOBA_BLOCK
    tm = min(_ROW_TILE, seq)
    tm_s = min(_ROW_TILE, dec_batch * tn)
    assert seq % tm == 0 and tm % MOBA_BLOCK == 0 and (dec_batch * tn) % tm_s == 0 and tm_s % CHUNK == 0
    n_pair = att_w // 128
    nb = seq // MOBA_BLOCK

    slopes = _alibi_slopes(n_heads)
    kpos, qpos = _position_tables(slopes, seq)
    group = jnp.arange(att_w) // HEAD_DIM
    gsum = (group[:, None] == group[None, :]).astype(_BF16)
    sample_tables = _sample_tables(slopes, tn, n_pages)
    rel = jnp.arange(MOBA_BLOCK)
    causal = jnp.stack([jnp.zeros((MOBA_BLOCK, MOBA_BLOCK), _F32),
                        jnp.where(rel[:, None] <= rel[None, :], 0.0, _NEG)])

    xp = x_prompt.reshape(batch * seq, d_model)
    xs = x_sample.reshape(dec_batch * tn, d_model)
    ck = cache_k.transpose(0, 1, 3, 4, 2).reshape(depth * n_pool, att_w, page_size)
    cv = cache_v.transpose(0, 1, 3, 4, 2).reshape(depth * n_pool, att_w, page_size)
    outs = [[] for _ in range(5)]
    for l in range(depth):
        wts = dict(
            ln1_g=ln1_g[l][None], w_in=w_in[l].astype(_BF16),
            qg=q_norm_g[l].reshape(1, att_w), kg=k_norm_g[l].reshape(1, att_w),
            glg=gm_ln_g[l].reshape(1, gm_w), glb=gm_ln_b[l].reshape(1, gm_w),
            gsum=gsum, gmo=gm_out_g[l][None], ag=att_out_g[l][None],
            wo_a=w_out[l][:att_w].astype(_BF16), wo_m=w_out[l][att_w:].astype(_BF16),
            ln2_g=ln2_g[l][None], w_up=w_up[l].astype(_BF16), w_down=w_down[l].astype(_BF16))
        w_tril = jnp.tril(w_s[l])
        wmix_p = w_tril.transpose(1, 0, 2).reshape(CHUNK, gm_heads * CHUNK).astype(_BF16)
        bmix_p = jnp.repeat(b_s[l].T, GM_HEAD_DIM, axis=1)
        eye = jnp.eye(CHUNK // tn, dtype=_F32)
        w_blk = jnp.einsum("ab,hij->haibj", eye, w_tril[:, :tn, :tn]).reshape(gm_heads, CHUNK, CHUNK)
        wmix_s = w_blk.transpose(1, 0, 2).reshape(CHUNK, gm_heads * CHUNK).astype(_BF16)
        bmix_s = jnp.tile(jnp.repeat(b_s[l][:, :tn].T, GM_HEAD_DIM, axis=1), (CHUNK // tn, 1))

        k_p, v_p, qt, qa, ka, va, ksum, mixn_p = _project(xp, seq, wts, wmix_p, bmix_p, True, tm,
                                                          (kpos, qpos))
        ksum = ksum.reshape(batch, nb, n_pair, 128).transpose(0, 2, 1, 3)
        a_t = _moba_prompt(qt, qa, ka, va, ksum, causal)
        xp = _merge_mlp(xp, a_t, mixn_p, wts, seq, tm, True)

        k_s, v_s, q_s, g_s, mixn_s = _project(xs, tn, wts, wmix_s, bmix_s, False, tm_s)
        a_s = _moba_sample(q_s, k_s, v_s, ck, cv, page_table + l * n_pool, sample_tables, tn)
        xs = _merge_mlp(xs, a_s, mixn_s, wts, tn, tm_s, False)

        outs[0].append(k_p.reshape(batch, n_heads, head_dim, seq).transpose(0, 3, 1, 2))
        outs[1].append(v_p.reshape(batch, n_heads, head_dim, seq).transpose(0, 3, 1, 2))
        outs[2].append(k_s.reshape(dec_batch, tn, n_heads, head_dim))
        outs[3].append(v_s.reshape(dec_batch, tn, n_heads, head_dim))
        outs[4].append(g_s.reshape(dec_batch, tn, gm_heads, gm_dim))
    return (xp.reshape(batch, seq, d_model), xs.reshape(dec_batch, tn, d_model),
            *(jnp.stack(o) for o in outs))
```

```python
import functools

import jax
import jax.numpy as jnp
from jax import lax
from jax.experimental import pallas as pl
from jax.experimental.pallas import tpu as pltpu

HEAD_DIM = 64
GM_HEAD_DIM = 64
MOBA_BLOCK = 256
MOBA_TOPK = 3
CHUNK = 128
PAGE_SIZE = 128
NORM_EPS = 1e-6

_F32 = jnp.float32
_BF16 = jnp.bfloat16
_NEG = -1e30
_VMEM_LIMIT = 56 * 1024 * 1024
_ROW_TILE = 512
_LOG2E = 1.4426950408889634
_V_ROWS = 80
_POS_TERMS = 3
_ITEM_PAIRS = 2
_SELECT_TILES = 2

_NT = (((1,), (1,)), ((), ()))
_TN = (((0,), (0,)), ((), ()))


def _dot(a, b):
    return jnp.dot(a, b, preferred_element_type=_F32)


def _split_bf16(x):
    hi = x.astype(_BF16)
    lo = (x - hi.astype(_F32)).astype(_BF16)
    return hi, lo


def _group_mean(x, gsum, width):
    hi, lo = _split_bf16(x)
    return (_dot(hi, gsum) + _dot(lo, gsum)) * (1.0 / width)


def _rms_rows(x, g):
    return x * lax.rsqrt(jnp.mean(x * x, axis=-1, keepdims=True) + NORM_EPS) * g


def _proj_kernel(x_ref, ln_g_ref, w_ref, qg_ref, kg_ref, glg_ref, glb_ref, gsum_ref,
                 wmix_ref, bmix_ref, gmo_ref, *refs, att_w, gm_w, prompt):
    if prompt:
        (kpos_ref, qpos_ref,
         k_ref, v_ref, qt_ref, qa_ref, ka_ref, va_ref, ksum_ref, mixn_ref) = refs
    else:
        k_ref, v_ref, q_ref, g_ref, mixn_ref = refs
    tm = x_ref.shape[0]
    gsum = gsum_ref[...]

    xn = _rms_rows(x_ref[...], ln_g_ref[...]).astype(_BF16)

    def zcols(lo, width):
        return _dot(xn, w_ref[:, lo:lo + width])

    def head_rms(z, g):
        return z * lax.rsqrt(_group_mean(z * z, gsum, HEAD_DIM) + NORM_EPS) * g

    q = head_rms(zcols(0, att_w), qg_ref[...])
    k = head_rms(zcols(att_w, att_w), kg_ref[...])
    v = zcols(2 * att_w, att_w)
    if prompt:
        vt = v.T
        k_ref[0] = k.T
        v_ref[0] = vt
        n_pair = att_w // 128
        qt = q.T
        qtb = (qt * (HEAD_DIM ** -0.5 * _LOG2E)).astype(_BF16)
        kb = k.astype(_BF16)
        vtb = vt.astype(_BF16)
        lane_half = lax.broadcasted_iota(jnp.int32, (MOBA_BLOCK, 128), 1) // HEAD_DIM
        row_half = lax.broadcasted_iota(jnp.int32, (128, tm), 0) // HEAD_DIM
        ones_row = (lax.broadcasted_iota(jnp.int32, (_V_ROWS - HEAD_DIM, MOBA_BLOCK), 0) == 0
                    ).astype(_BF16)
        for h in range(att_w // HEAD_DIM):
            pair = slice((h // 2) * 128, (h // 2 + 1) * 128)
            qa = jnp.where(row_half == h % 2, qtb[pair], qpos_ref[h])
            for j in range(tm // MOBA_BLOCK):
                blk = slice(j * MOBA_BLOCK, (j + 1) * MOBA_BLOCK)
                qa_ref[0, h, j] = qa[:, blk]
                ka_ref[0, h, j] = jnp.where(lane_half == h % 2, kb[blk, pair], kpos_ref[h, j])
                va_ref[0, h, j, 0:HEAD_DIM, :] = vtb[h * HEAD_DIM:(h + 1) * HEAD_DIM, blk]
                va_ref[0, h, j, HEAD_DIM:_V_ROWS, :] = ones_row
        for j in range(tm // MOBA_BLOCK):
            blk = slice(j * MOBA_BLOCK, (j + 1) * MOBA_BLOCK)
            qt_ref[0, :, j] = qt[:, blk].reshape(n_pair, 128, MOBA_BLOCK)
            ksum_ref[0, j:j + 1, :] = jnp.sum(k[blk], axis=0, keepdims=True)
    else:
        k_ref[...] = k
        v_ref[...] = v
        q_ref[...] = q

    u = jax.nn.gelu(zcols(3 * att_w, gm_w))
    gg = jax.nn.gelu(zcols(3 * att_w + gm_w, gm_w))
    mu = _group_mean(gg, gsum, GM_HEAD_DIM)
    cen = gg - mu
    var = _group_mean(cen * cen, gsum, GM_HEAD_DIM)
    g = cen * lax.rsqrt(var + NORM_EPS) * glg_ref[...] + glb_ref[...]
    if not prompt:
        g_ref[...] = g

    n_grp = gm_w // GM_HEAD_DIM
    gb = g.astype(_BF16)
    wmix = wmix_ref[...]
    bmix = bmix_ref[...]
    lane_grp = lax.broadcasted_iota(jnp.int32, (CHUNK, gm_w), 1) // GM_HEAD_DIM
    pieces = []
    for c in range(tm // CHUNK):
        rows = slice(c * CHUNK, (c + 1) * CHUNK)
        stack = jnp.concatenate(
            [jnp.where(lane_grp == h, gb[rows], jnp.zeros_like(gb[rows])) for h in range(n_grp)],
            axis=0)
        mixed = _dot(wmix, stack) + bmix
        pieces.append(u[rows] * mixed)
    m = jnp.concatenate(pieces, axis=0)
    mixn_ref[...] = _rms_rows(m, gmo_ref[...]).astype(_BF16)


def _project(x2d, seq, wts, wmix, bmix, prompt, tm, tables=()):
    n, d_model = x2d.shape
    att_w = wts["qg"].shape[1]
    gm_w = wts["glg"].shape[1]
    steps = n // tm
    per_seq = seq // tm if prompt else 1
    row = lambda i: (i, 0)
    full2 = lambda i: (0, 0)
    in_specs = [
        pl.BlockSpec((tm, d_model), row),
        pl.BlockSpec((1, d_model), full2),
        pl.BlockSpec(wts["w_in"].shape, full2),
        pl.BlockSpec((1, att_w), full2),
        pl.BlockSpec((1, att_w), full2),
        pl.BlockSpec((1, gm_w), full2),
        pl.BlockSpec((1, gm_w), full2),
        pl.BlockSpec(wts["gsum"].shape, full2),
        pl.BlockSpec(wmix.shape, full2),
        pl.BlockSpec(bmix.shape, full2),
        pl.BlockSpec((1, gm_w), full2),
    ]
    f32_rows = jax.ShapeDtypeStruct((n, att_w), _F32)
    if prompt:
        batch = n // seq
        n_pair = att_w // 128
        n_heads = att_w // HEAD_DIM
        blk_per_tile = tm // MOBA_BLOCK
        nb = seq // MOBA_BLOCK
        in_specs += [
            pl.BlockSpec((n_heads, blk_per_tile, MOBA_BLOCK, 128), lambda i: (0, i % per_seq, 0, 0)),
            pl.BlockSpec((n_heads, 128, tm), lambda i: (0, 0, i % per_seq)),
        ]
        blocks = lambda i: (i // per_seq, 0, i % per_seq, 0, 0)
        kv_t = jax.ShapeDtypeStruct((batch, att_w, seq), _F32)
        kv_t_spec = pl.BlockSpec((1, att_w, tm), lambda i: (i // per_seq, 0, i % per_seq))
        out_shape = (
            kv_t, kv_t,
            jax.ShapeDtypeStruct((batch, n_pair, nb, 128, MOBA_BLOCK), _F32),
            jax.ShapeDtypeStruct((batch, n_heads, nb, 128, MOBA_BLOCK), _BF16),
            jax.ShapeDtypeStruct((batch, n_heads, nb, MOBA_BLOCK, 128), _BF16),
            jax.ShapeDtypeStruct((batch, n_heads, nb, _V_ROWS, MOBA_BLOCK), _BF16),
            jax.ShapeDtypeStruct((steps, blk_per_tile, att_w), _F32),
            jax.ShapeDtypeStruct((n, gm_w), _BF16),
        )
        out_specs = (
            kv_t_spec, kv_t_spec,
            pl.BlockSpec((1, n_pair, blk_per_tile, 128, MOBA_BLOCK), blocks),
            pl.BlockSpec((1, n_heads, blk_per_tile, 128, MOBA_BLOCK), blocks),
            pl.BlockSpec((1, n_heads, blk_per_tile, MOBA_BLOCK, 128), blocks),
            pl.BlockSpec((1, n_heads, blk_per_tile, _V_ROWS, MOBA_BLOCK), blocks),
            pl.BlockSpec((1, blk_per_tile, att_w), lambda i: (i, 0, 0)),
            pl.BlockSpec((tm, gm_w), row),
        )
    else:
        out_shape = (f32_rows, f32_rows, f32_rows,
                     jax.ShapeDtypeStruct((n, gm_w), _F32),
                     jax.ShapeDtypeStruct((n, gm_w), _BF16))
        out_specs = tuple(pl.BlockSpec((tm, att_w), row) for _ in range(5))
    return pl.pallas_call(
        functools.partial(_proj_kernel, att_w=att_w, gm_w=gm_w, prompt=prompt),
        grid=(steps,),
        in_specs=in_specs,
        out_specs=out_specs,
        out_shape=out_shape,
        compiler_params=pltpu.CompilerParams(dimension_semantics=("arbitrary",),
                                             vmem_limit_bytes=_VMEM_LIMIT),
        name="project_prompt" if prompt else "project_sample",
    )(x2d, wts["ln1_g"], wts["w_in"], wts["qg"], wts["kg"], wts["glg"], wts["glb"], wts["gsum"],
      wmix, bmix, wts["gmo"], *tables)


def _mlp_kernel(x_ref, a_ref, mixn_ref, ag_ref, wo_a_ref, wo_m_ref, ln2_ref, wup_ref, wdn_ref,
                o_ref, *, a_transposed, ff_chunk):
    if a_transposed:
        att_w = ag_ref.shape[1]
        a = jnp.concatenate([a_ref[0, :, j].reshape(att_w, MOBA_BLOCK).T
                             for j in range(a_ref.shape[2])], axis=0)
    else:
        a = a_ref[...]
    an = _rms_rows(a, ag_ref[...]).astype(_BF16)
    x1 = x_ref[...] + _dot(an, wo_a_ref[...]) + _dot(mixn_ref[...], wo_m_ref[...])
    hn = _rms_rows(x1, ln2_ref[...]).astype(_BF16)
    y = None
    for c in range(wup_ref.shape[1] // ff_chunk):
        cols = slice(c * ff_chunk, (c + 1) * ff_chunk)
        h = jnp.square(jnp.maximum(_dot(hn, wup_ref[:, cols]), 0.0)).astype(_BF16)
        d = _dot(h, wdn_ref[cols, :])
        y = d if y is None else y + d
    o_ref[...] = x1 + y


def _merge_mlp(x2d, a, mixn, wts, seq, tm, a_transposed):
    n, d_model = x2d.shape
    att_w = wts["wo_a"].shape[0]
    gm_w = wts["wo_m"].shape[0]
    d_ff = wts["w_up"].shape[1]
    row = lambda i: (i, 0)
    full2 = lambda i: (0, 0)
    once = dict(pipeline_mode=pl.Buffered(1))
    if a_transposed:
        per_seq = seq // tm
        a_spec = pl.BlockSpec((1, att_w // 128, tm // MOBA_BLOCK, 128, MOBA_BLOCK),
                              lambda i: (i // per_seq, 0, i % per_seq, 0, 0))
    else:
        a_spec = pl.BlockSpec((tm, att_w), row)
    return pl.pallas_call(
        functools.partial(_mlp_kernel, a_transposed=a_transposed, ff_chunk=min(d_ff, 1024)),
        grid=(n // tm,),
        in_specs=[
            pl.BlockSpec((tm, d_model), row),
            a_spec,
            pl.BlockSpec((tm, gm_w), row),
            pl.BlockSpec((1, att_w), full2),
            pl.BlockSpec((att_w, d_model), full2, **once),
            pl.BlockSpec((gm_w, d_model), full2, **once),
            pl.BlockSpec((1, d_model), full2),
            pl.BlockSpec((d_model, d_ff), full2, **once),
            pl.BlockSpec((d_ff, d_model), full2, **once),
        ],
        out_specs=pl.BlockSpec((tm, d_model), row),
        out_shape=jax.ShapeDtypeStruct((n, d_model), _F32),
        compiler_params=pltpu.CompilerParams(dimension_semantics=("arbitrary",),
                                             vmem_limit_bytes=_VMEM_LIMIT),
        name="merge_mlp_prompt" if a_transposed else "merge_mlp_sample",
    )(x2d, a, mixn, wts["ag"], wts["wo_a"], wts["wo_m"], wts["ln2_g"], wts["w_up"], wts["w_down"])


def _topk_rows(gate, n_valid, k):
    nb = gate.shape[0]
    rows = lax.broadcasted_iota(jnp.int32, gate.shape, 0)
    valid = rows < n_valid
    g = jnp.where(valid, gate, -jnp.inf)
    sel = jnp.zeros(gate.shape, jnp.bool_)
    for _ in range(min(k, nb)):
        best = jnp.max(g, axis=0, keepdims=True)
        first = jnp.min(jnp.where(g == best, rows, nb), axis=0, keepdims=True)
        hit = rows == first
        sel = jnp.logical_or(sel, hit)
        g = jnp.where(hit, -jnp.inf, g)
    return jnp.logical_and(sel, valid)


def _moba_prompt_kernel(qt_ref, qa_ref, ka_ref, va_ref, ksum_ref, causal_ref, o_ref,
                        sel_ref, m_ref, alpha_ref, acc_ref, t_ref, p_ref):
    nb = ka_ref.shape[2]
    tq = MOBA_BLOCK
    n_total = sum((i + 2) // 2 for i in range(nb))

    kmean = ksum_ref[0, 0] * (1.0 / MOBA_BLOCK)
    lane_head = lax.broadcasted_iota(jnp.int32, kmean.shape, 1) // HEAD_DIM
    kmean2 = jnp.concatenate([jnp.where(lane_head == r, kmean, 0.0) for r in range(2)], axis=0)

    def select(step, carry):
        for u in range(_SELECT_TILES):
            i = step * _SELECT_TILES + u
            gate = jnp.dot(kmean2, qt_ref[0, 0, i], precision=lax.Precision.HIGHEST,
                           preferred_element_type=_F32)
            for r in range(2):
                sel_ref[i, r] = _topk_rows(gate[r * nb:(r + 1) * nb], i, MOBA_TOPK).astype(_F32)
        return carry

    lax.fori_loop(0, nb // _SELECT_TILES, select, 0)

    ones_row = jnp.where(lax.broadcasted_iota(jnp.int32, (_V_ROWS, tq), 0) == HEAD_DIM, 1.0, 0.0)
    for r in range(2):
        m_ref[r] = jnp.full((1, tq), _NEG, _F32)
        acc_ref[r] = ones_row
    alpha_ref[...] = jnp.ones(alpha_ref.shape, _F32)
    p_ref[1] = jnp.zeros(p_ref.shape[1:], p_ref.dtype)

    def advance(i, k):
        last = k + 1 >= lax.div(i + 2, 2)
        return jnp.where(last, i + 1, i), jnp.where(last, 0, k + 1)

    def blocks(i, k):
        tile = jnp.minimum(i, nb - 1)
        first = jnp.where(k == 0, tile, jnp.minimum(2 * k - 1, nb - 1))
        return tile, first, jnp.minimum(2 * k, nb - 1)

    def scores(item, slot):
        tile, *ns = blocks(*item)
        for r in range(2):
            for d, n in enumerate(ns):
                t_ref[slot, r, d] = _dot(ka_ref[0, r, n], qa_ref[0, r, tile])

    def softmax(item, slot):
        i, k = item
        tile, n0, n1 = blocks(i, k)
        real = i < nb
        own = jnp.logical_and(k == 0, real)
        mask = causal_ref[own.astype(jnp.int32)]
        for r in range(2):
            t0 = t_ref[slot, r, 0] + mask
            t1 = t_ref[slot, r, 1]
            c0 = jnp.logical_or(jnp.logical_and(sel_ref[tile, r, pl.ds(n0, 1), :] > 0.5, real), own)
            c1 = jnp.logical_and(sel_ref[tile, r, pl.ds(n1, 1), :] > 0.5, real)
            m = jnp.where(k == 0, _NEG, m_ref[r])
            m_new = jnp.maximum(m, jnp.maximum(
                jnp.where(c0, jnp.max(t0, axis=0, keepdims=True), _NEG),
                jnp.where(c1, jnp.max(t1, axis=0, keepdims=True), _NEG)))
            p_ref[slot, r, 0] = jnp.exp2(t0 - jnp.where(c0, m_new, -_NEG)).astype(_BF16)
            p_ref[slot, r, 1] = jnp.exp2(t1 - jnp.where(c1, m_new, -_NEG)).astype(_BF16)
            alpha_ref[slot, r] = jnp.where(real, jnp.exp2(m - m_new), 1.0)
            m_ref[r] = m_new

    def weighted_values(item, slot):
        tile, n0, n1 = blocks(*item)
        for r in range(2):
            acc = (alpha_ref[slot, r] * acc_ref[r]
                   + _dot(va_ref[0, r, n0], p_ref[slot, r, 0])
                   + _dot(va_ref[0, r, n1], p_ref[slot, r, 1]))
            acc_ref[r] = acc
            o_ref[0, 0, tile, r * HEAD_DIM:(r + 1) * HEAD_DIM, :] = (
                acc[0:HEAD_DIM] / acc[HEAD_DIM:HEAD_DIM + 1])

    def step(_, carry):
        before, item = carry
        for _ in range(_ITEM_PAIRS):
            after = advance(*item)
            after2 = advance(*after)
            weighted_values(before, 1)
            softmax(item, 0)
            scores(after, 1)
            weighted_values(item, 0)
            scores(after2, 0)
            softmax(after, 1)
            before, item = after, after2
        return before, item

    first = (jnp.int32(0), jnp.int32(0))
    nothing = (jnp.int32(nb), jnp.int32(1))
    scores(first, 0)
    last, _ = lax.fori_loop(0, pl.cdiv(n_total, 2 * _ITEM_PAIRS), step, (nothing, first))
    weighted_values(last, 1)


def _moba_prompt(qt, qa, ka, va, ksum, causal):
    batch, n_pair, nb = qt.shape[:3]
    assert nb % _SELECT_TILES == 0
    tq = MOBA_BLOCK
    per_pair = lambda b, p: (b, p, 0, 0, 0)
    return pl.pallas_call(
        _moba_prompt_kernel,
        grid=(batch, n_pair),
        in_specs=[
            pl.BlockSpec((1, 1, nb, 128, tq), per_pair),
            pl.BlockSpec((1, 2, nb, 128, tq), per_pair),
            pl.BlockSpec((1, 2, nb, MOBA_BLOCK, 128), per_pair),
            pl.BlockSpec((1, 2, nb, _V_ROWS, MOBA_BLOCK), per_pair),
            pl.BlockSpec((1, 1, nb, 128), lambda b, p: (b, p, 0, 0)),
            pl.BlockSpec((2, MOBA_BLOCK, tq), lambda b, p: (0, 0, 0)),
        ],
        out_specs=pl.BlockSpec((1, 1, nb, 128, tq), per_pair),
        out_shape=jax.ShapeDtypeStruct((batch, n_pair, nb, 128, tq), _F32),
        scratch_shapes=[pltpu.VMEM((nb, 2, nb, tq), _F32),
                        pltpu.VMEM((2, 1, tq), _F32),
                        pltpu.VMEM((2, 2, 1, tq), _F32),
                        pltpu.VMEM((2, _V_ROWS, tq), _F32),
                        pltpu.VMEM((2, 2, 2, MOBA_BLOCK, tq), _F32),
                        pltpu.VMEM((2, 2, 2, MOBA_BLOCK, tq), _BF16)],
        compiler_params=pltpu.CompilerParams(
            dimension_semantics=("arbitrary", "arbitrary"),
            vmem_limit_bytes=_VMEM_LIMIT),
        name="moba_prompt",
    )(qt, qa, ka, va, ksum, causal)


_RING = 64
_AHEAD = 56
_PAGE_GROUP = 8


def _moba_sample_kernel(pt_ref, q_ref, kn_ref, vn_ref, alibi_ref, newbias_ref,
                        ck_ref, cv_ref, o_ref, kring_ref, vring_ref, ksem_ref, vsem_ref,
                        s_ref, *, n_heads):
    b = pl.program_id(0)
    n_seq = pl.num_programs(0)
    n_pages = pt_ref.shape[1]
    tn, att_w = q_ref.shape
    q_rows = n_heads * tn
    ppb = MOBA_BLOCK // PAGE_SIZE
    nb = n_pages // ppb
    i0 = b * n_pages

    def copy(src_ref, ring_ref, sem_ref, page, slot):
        return pltpu.make_async_copy(src_ref.at[page], ring_ref.at[slot], sem_ref.at[slot])

    def fetch(src_ref, ring_ref, sem_ref, i):
        seq = jnp.minimum(i // n_pages, n_seq - 1)
        copy(src_ref, ring_ref, sem_ref, pt_ref[seq, i % n_pages], i % _RING).start()

    def wait(src_ref, ring_ref, sem_ref, i):
        copy(src_ref, ring_ref, sem_ref, 0, i % _RING).wait()

    k_stream = (ck_ref, kring_ref, ksem_ref)
    v_stream = (cv_ref, vring_ref, vsem_ref)

    @pl.when(b == 0)
    def _():
        for i in range(_AHEAD):
            fetch(*k_stream, jnp.int32(i))
            fetch(*v_stream, jnp.int32(i))

    q = q_ref[...]
    qrow = lax.broadcasted_iota(jnp.int32, (q_rows, att_w), 0)
    qlane = lax.broadcasted_iota(jnp.int32, (q_rows, att_w), 1)
    qbd = jnp.where(qrow // tn == qlane // HEAD_DIM, jnp.concatenate([q] * n_heads, axis=0), 0.0)
    qbd = (qbd * HEAD_DIM ** -0.5).astype(_BF16)
    lane = lax.broadcasted_iota(jnp.int32, (q_rows, 128), 1)

    def k_pages(step, gate):
        first = step * _PAGE_GROUP
        for u in range(_PAGE_GROUP):
            wait(*k_stream, i0 + first + u)
        for u in range(_PAGE_GROUP):
            slot = (i0 + first + u) % _RING
            s = _dot(qbd, kring_ref[slot].astype(_BF16))
            s_ref[first + u] = s
            gate = gate + jnp.where(lane == (first + u) // ppb,
                                    jnp.sum(s, axis=1, keepdims=True), 0.0)
        for u in range(_PAGE_GROUP):
            fetch(*k_stream, i0 + first + u + _AHEAD)
        return gate

    gate = lax.fori_loop(0, n_pages // _PAGE_GROUP, k_pages, jnp.zeros((q_rows, 128), _F32))

    gate = jnp.where(lane < nb, gate, -jnp.inf)
    sel = jnp.zeros(gate.shape, jnp.bool_)
    for _ in range(min(MOBA_TOPK, nb)):
        best = jnp.max(gate, axis=1, keepdims=True)
        first = jnp.min(jnp.where(gate == best, lane, 128), axis=1, keepdims=True)
        hit = lane == first
        sel = jnp.logical_or(sel, hit)
        gate = jnp.where(hit, -jnp.inf, gate)
    sel = jnp.where(jnp.logical_and(sel, lane < nb), 1.0, 0.0)

    slope = alibi_ref[0]
    origin = alibi_ref[1]

    def logits_pages(step, run_max):
        for u in range(_PAGE_GROUP):
            j = step * _PAGE_GROUP + u
            pos = (lane + j * PAGE_SIZE).astype(_F32)
            chosen = jnp.sum(jnp.where(lane == j // ppb, sel, 0.0), axis=1, keepdims=True) > 0.5
            logit = jnp.where(chosen, s_ref[j] + (slope * pos + origin), _NEG)
            s_ref[j] = logit
            run_max = jnp.maximum(run_max, logit)
        return run_max

    run_max = lax.fori_loop(0, n_pages // _PAGE_GROUP, logits_pages,
                            jnp.full((q_rows, 128), _NEG, _F32))
    logit_new = lax.dot_general(qbd, kn_ref[...].astype(_BF16), _NT,
                                preferred_element_type=_F32) + newbias_ref[...]
    m = jnp.maximum(jnp.max(run_max, axis=1, keepdims=True), jnp.max(logit_new, axis=1, keepdims=True))
    m_tile = jnp.broadcast_to(m, (q_rows, 128))

    def exp_pages(step, run_sum):
        for u in range(_PAGE_GROUP):
            j = step * _PAGE_GROUP + u
            p = jnp.exp(s_ref[j] - m_tile)
            s_ref[j] = p
            run_sum = run_sum + p
        return run_sum

    run_sum = lax.fori_loop(0, n_pages // _PAGE_GROUP, exp_pages, jnp.zeros((q_rows, 128), _F32))
    p_new = jnp.exp(logit_new - m)
    inv = 1.0 / (jnp.sum(run_sum, axis=1, keepdims=True) + jnp.sum(p_new, axis=1, keepdims=True))
    inv_tile = jnp.broadcast_to(inv, (q_rows, 128))

    def v_pages(step, acc):
        first = step * _PAGE_GROUP
        for u in range(_PAGE_GROUP):
            wait(*v_stream, i0 + first + u)
        for u in range(_PAGE_GROUP):
            slot = (i0 + first + u) % _RING
            p = (s_ref[first + u] * inv_tile).astype(_BF16)
            acc = acc + lax.dot_general(p, vring_ref[slot].astype(_BF16), _NT,
                                        preferred_element_type=_F32)
        for u in range(_PAGE_GROUP):
            fetch(*v_stream, i0 + first + u + _AHEAD)
        return acc

    acc = _dot((p_new * inv).astype(_BF16), vn_ref[...].astype(_BF16))
    acc = lax.fori_loop(0, n_pages // _PAGE_GROUP, v_pages, acc)

    @pl.when(b == n_seq - 1)
    def _():
        for i in range(_AHEAD):
            wait(*k_stream, n_seq * n_pages + i)
            wait(*v_stream, n_seq * n_pages + i)

    lane_head = lax.broadcasted_iota(jnp.int32, (tn, att_w), 1) // HEAD_DIM
    out = jnp.zeros((tn, att_w), _F32)
    for h in range(n_heads):
        out = out + jnp.where(lane_head == h, acc[h * tn:(h + 1) * tn, :], 0.0)
    o_ref[...] = out


def _sample_tables(slopes, tn, n_pages):
    n_heads = slopes.shape[0]
    n_keys = n_pages * PAGE_SIZE
    row = jnp.arange(n_heads * tn)
    slope_row = slopes[row // tn][:, None]
    tok = (row % tn).astype(_F32)[:, None]
    alibi = jnp.stack([jnp.broadcast_to(slope_row, (n_heads * tn, 128)),
                       jnp.broadcast_to(-slope_row * (float(n_keys) + tok), (n_heads * tn, 128))])
    dist = tok - jnp.arange(tn, dtype=_F32)[None, :]
    newbias = jnp.where(dist >= 0.0, -slope_row * dist, _NEG)
    return alibi, newbias


def _moba_sample(q, k_new, v_new, cache_k, cache_v, pages, tables, tn):
    n, att_w = q.shape
    n_heads = att_w // HEAD_DIM
    n_seq, n_pages = pages.shape
    assert n_pages % _PAGE_GROUP == 0 and _AHEAD + _PAGE_GROUP <= _RING
    row = lambda b, pt: (b, 0)
    return pl.pallas_call(
        functools.partial(_moba_sample_kernel, n_heads=n_heads),
        grid_spec=pltpu.PrefetchScalarGridSpec(
            num_scalar_prefetch=1,
            grid=(n_seq,),
            in_specs=[
                pl.BlockSpec((tn, att_w), row),
                pl.BlockSpec((tn, att_w), row),
                pl.BlockSpec((tn, att_w), row),
                *(pl.BlockSpec(t.shape, lambda b, pt, nd=t.ndim: (0,) * nd) for t in tables),
                pl.BlockSpec(memory_space=pl.ANY),
                pl.BlockSpec(memory_space=pl.ANY),
            ],
            out_specs=pl.BlockSpec((tn, att_w), row),
            scratch_shapes=[
                pltpu.VMEM((_RING, att_w, PAGE_SIZE), _F32),
                pltpu.VMEM((_RING, att_w, PAGE_SIZE), _F32),
                pltpu.SemaphoreType.DMA((_RING,)),
                pltpu.SemaphoreType.DMA((_RING,)),
                pltpu.VMEM((n_pages, n_heads * tn, PAGE_SIZE), _F32),
            ],
        ),
        out_shape=jax.ShapeDtypeStruct((n, att_w), _F32),
        compiler_params=pltpu.CompilerParams(dimension_semantics=("arbitrary",),
                                             vmem_limit_bytes=_VMEM_LIMIT),
        name="moba_sample",
    )(pages, q, k_new, v_new, *tables, cache_k, cache_v)


def _alibi_slopes(n_heads):
    h = jnp.arange(1, n_heads + 1, dtype=_F32)
    return jnp.exp2(-8.0 * h / n_heads)


def _split_terms(x):
    terms = []
    for _ in range(_POS_TERMS):
        bits = lax.bitcast_convert_type(x, jnp.uint32) & jnp.uint32(0xFFFF0000)
        t = lax.bitcast_convert_type(bits, _F32)
        terms.append(t.astype(_BF16))
        x = x - t
    return terms


def _position_tables(slopes, seq):
    n_heads = slopes.shape[0]
    pos = jnp.arange(seq, dtype=_F32)
    val = (slopes * _LOG2E)[:, None] * pos[None, :]
    start = HEAD_DIM * (1 - jnp.arange(n_heads) % 2)
    slot = jnp.arange(128)[None, :] - start[:, None]

    def table(terms, first, slot):
        out = jnp.where(jnp.logical_and(slot >= 0, slot < 2 * _POS_TERMS), 1.0, 0.0)
        for n, term in enumerate(terms):
            out = jnp.where(slot == first + n, term.astype(_F32), out)
        return out.astype(_BF16)

    kpos = table([t[:, :, None] for t in _split_terms(val)], 0, slot[:, None, :])
    qpos = table([t[:, None, :] for t in _split_terms(-val)], _POS_TERMS, slot[:, :, None])
    return kpos.reshape(n_heads, seq // MOBA_BLOCK, MOBA_BLOCK, 128), qpos


def kernel(x_prompt, x_sample, cache_k, cache_v, page_table, ln1_g, w_in, q_norm_g, k_norm_g,
           gm_ln_g, gm_ln_b, w_s, b_s, att_out_g, gm_out_g, w_out, ln2_g, w_up, w_down):
    batch, seq, d_model = x_prompt.shape
    dec_batch, tn, _ = x_sample.shape
    depth, n_pool, page_size, n_heads, head_dim = cache_k.shape
    gm_heads, gm_dim = gm_ln_g.shape[1:]
    att_w = n_heads * head_dim
    gm_w = gm_heads * gm_dim
    n_pages = page_table.shape[1]
    assert (head_dim, gm_dim, page_size, w_s.shape[-1]) == (HEAD_DIM, GM_HEAD_DIM, PAGE_SIZE, CHUNK)
    assert att_w == gm_w and att_w % 128 == 0 and n_heads * tn <= 128 and 128 % n_heads == 0
    assert seq % MOBA_BLOCK == 0 and CHUNK % tn == 0
    assert (n_pages * PAGE_SIZE) % MOBA_BLOCK == 0 and n_pages * PAGE_SIZE >= MOBA_BLOCK
    tm = min(_ROW_TILE, seq)
    tm_s = min(_ROW_TILE, dec_batch * tn)
    assert seq % tm == 0 and tm % MOBA_BLOCK == 0 and (dec_batch * tn) % tm_s == 0 and tm_s % CHUNK == 0
    n_pair = att_w // 128
    nb = seq // MOBA_BLOCK

    slopes = _alibi_slopes(n_heads)
    kpos, qpos = _position_tables(slopes, seq)
    group = jnp.arange(att_w) // HEAD_DIM
    gsum = (group[:, None] == group[None, :]).astype(_BF16)
    sample_tables = _sample_tables(slopes, tn, n_pages)
    rel = jnp.arange(MOBA_BLOCK)
    causal = jnp.stack([jnp.zeros((MOBA_BLOCK, MOBA_BLOCK), _F32),
                        jnp.where(rel[:, None] <= rel[None, :], 0.0, _NEG)])

    xp = x_prompt.reshape(batch * seq, d_model)
    xs = x_sample.reshape(dec_batch * tn, d_model)
    ck = cache_k.transpose(0, 1, 3, 4, 2).reshape(depth * n_pool, att_w, page_size)
    cv = cache_v.transpose(0, 1, 3, 4, 2).reshape(depth * n_pool, att_w, page_size)
    outs = [[] for _ in range(5)]
    for l in range(depth):
        wts = dict(
            ln1_g=ln1_g[l][None], w_in=w_in[l].astype(_BF16),
            qg=q_norm_g[l].reshape(1, att_w), kg=k_norm_g[l].reshape(1, att_w),
            glg=gm_ln_g[l].reshape(1, gm_w), glb=gm_ln_b[l].reshape(1, gm_w),
            gsum=gsum, gmo=gm_out_g[l][None], ag=att_out_g[l][None],
            wo_a=w_out[l][:att_w].astype(_BF16), wo_m=w_out[l][att_w:].astype(_BF16),
            ln2_g=ln2_g[l][None], w_up=w_up[l].astype(_BF16), w_down=w_down[l].astype(_BF16))
        w_tril = jnp.tril(w_s[l])
        wmix_p = w_tril.transpose(1, 0, 2).reshape(CHUNK, gm_heads * CHUNK).astype(_BF16)
        bmix_p = jnp.repeat(b_s[l].T, GM_HEAD_DIM, axis=1)
        eye = jnp.eye(CHUNK // tn, dtype=_F32)
        w_blk = jnp.einsum("ab,hij->haibj", eye, w_tril[:, :tn, :tn]).reshape(gm_heads, CHUNK, CHUNK)
        wmix_s = w_blk.transpose(1, 0, 2).reshape(CHUNK, gm_heads * CHUNK).astype(_BF16)
        bmix_s = jnp.tile(jnp.repeat(b_s[l][:, :tn].T, GM_HEAD_DIM, axis=1), (CHUNK // tn, 1))

        k_p, v_p, qt, qa, ka, va, ksum, mixn_p = _project(xp, seq, wts, wmix_p, bmix_p, True, tm,
                                                          (kpos, qpos))
        ksum = ksum.reshape(batch, nb, n_pair, 128).transpose(0, 2, 1, 3)
        a_t = _moba_prompt(qt, qa, ka, va, ksum, causal)
        xp = _merge_mlp(xp, a_t, mixn_p, wts, seq, tm, True)

        k_s, v_s, q_s, g_s, mixn_s = _project(xs, tn, wts, wmix_s, bmix_s, False, tm_s)
        a_s = _moba_sample(q_s, k_s, v_s, ck, cv, page_table + l * n_pool, sample_tables, tn)
        xs = _merge_mlp(xs, a_s, mixn_s, wts, tn, tm_s, False)

        outs[0].append(k_p.reshape(batch, n_heads, head_dim, seq).transpose(0, 3, 1, 2))
        outs[1].append(v_p.reshape(batch, n_heads, head_dim, seq).transpose(0, 3, 1, 2))
        outs[2].append(k_s.reshape(dec_batch, tn, n_heads, head_dim))
        outs[3].append(v_s.reshape(dec_batch, tn, n_heads, head_dim))
        outs[4].append(g_s.reshape(dec_batch, tn, gm_heads, gm_dim))
    return (xp.reshape(batch, seq, d_model), xs.reshape(dec_batch, tn, d_model),
            *(jnp.stack(o) for o in outs))
```

```python
import functools

import jax
import jax.numpy as jnp
from jax import lax
from jax.experimental import pallas as pl
from jax.experimental.pallas import tpu as pltpu

HEAD_DIM = 64
GM_HEAD_DIM = 64
MOBA_BLOCK = 256
MOBA_TOPK = 3
CHUNK = 128
PAGE_SIZE = 128
NORM_EPS = 1e-6

_F32 = jnp.float32
_BF16 = jnp.bfloat16
_NEG = -1e30
_VMEM_LIMIT = 56 * 1024 * 1024
_ROW_TILE = 512
_LOG2E = 1.4426950408889634
_V_ROWS = 80
_POS_TERMS = 3
_ITEM_PAIRS = 2
_SELECT_TILES = 2

_NT = (((1,), (1,)), ((), ()))
_TN = (((0,), (0,)), ((), ()))


def _dot(a, b):
    return jnp.dot(a, b, preferred_element_type=_F32)


def _split_bf16(x):
    hi = x.astype(_BF16)
    lo = (x - hi.astype(_F32)).astype(_BF16)
    return hi, lo


def _group_mean(x, gsum, width):
    hi, lo = _split_bf16(x)
    return (_dot(hi, gsum) + _dot(lo, gsum)) * (1.0 / width)


def _rms_rows(x, g):
    return x * lax.rsqrt(jnp.mean(x * x, axis=-1, keepdims=True) + NORM_EPS) * g


def _proj_kernel(x_ref, ln_g_ref, w_ref, qg_ref, kg_ref, glg_ref, glb_ref, gsum_ref,
                 wmix_ref, bmix_ref, gmo_ref, *refs, att_w, gm_w, prompt):
    if prompt:
        (kpos_ref, qpos_ref,
         k_ref, v_ref, qt_ref, qa_ref, ka_ref, va_ref, ksum_ref, mixn_ref) = refs
    else:
        k_ref, v_ref, q_ref, g_ref, mixn_ref = refs
    tm = x_ref.shape[0]
    gsum = gsum_ref[...]

    xn = _rms_rows(x_ref[...], ln_g_ref[...]).astype(_BF16)

    def zcols(lo, width):
        return _dot(xn, w_ref[:, lo:lo + width])

    def head_rms(z, g):
        return z * lax.rsqrt(_group_mean(z * z, gsum, HEAD_DIM) + NORM_EPS) * g

    q = head_rms(zcols(0, att_w), qg_ref[...])
    k = head_rms(zcols(att_w, att_w), kg_ref[...])
    v = zcols(2 * att_w, att_w)
    if prompt:
        vt = v.T
        k_ref[0] = k.T
        v_ref[0] = vt
        n_pair = att_w // 128
        qt = q.T
        qtb = (qt * (HEAD_DIM ** -0.5 * _LOG2E)).astype(_BF16)
        kb = k.astype(_BF16)
        vtb = vt.astype(_BF16)
        lane_half = lax.broadcasted_iota(jnp.int32, (MOBA_BLOCK, 128), 1) // HEAD_DIM
        row_half = lax.broadcasted_iota(jnp.int32, (128, tm), 0) // HEAD_DIM
        ones_row = (lax.broadcasted_iota(jnp.int32, (_V_ROWS - HEAD_DIM, MOBA_BLOCK), 0) == 0
                    ).astype(_BF16)
        for h in range(att_w // HEAD_DIM):
            pair = slice((h // 2) * 128, (h // 2 + 1) * 128)
            qa = jnp.where(row_half == h % 2, qtb[pair], qpos_ref[h])
            for j in range(tm // MOBA_BLOCK):
                blk = slice(j * MOBA_BLOCK, (j + 1) * MOBA_BLOCK)
                qa_ref[0, h, j] = qa[:, blk]
                ka_ref[0, h, j] = jnp.where(lane_half == h % 2, kb[blk, pair], kpos_ref[h, j])
                va_ref[0, h, j, 0:HEAD_DIM, :] = vtb[h * HEAD_DIM:(h + 1) * HEAD_DIM, blk]
                va_ref[0, h, j, HEAD_DIM:_V_ROWS, :] = ones_row
        for j in range(tm // MOBA_BLOCK):
            blk = slice(j * MOBA_BLOCK, (j + 1) * MOBA_BLOCK)
            qt_ref[0, :, j] = qt[:, blk].reshape(n_pair, 128, MOBA_BLOCK)
            ksum_ref[0, j:j + 1, :] = jnp.sum(k[blk], axis=0, keepdims=True)
    else:
        k_ref[...] = k
        v_ref[...] = v
        q_ref[...] = q

    u = jax.nn.gelu(zcols(3 * att_w, gm_w))
    gg = jax.nn.gelu(zcols(3 * att_w + gm_w, gm_w))
    mu = _group_mean(gg, gsum, GM_HEAD_DIM)
    cen = gg - mu
    var = _group_mean(cen * cen, gsum, GM_HEAD_DIM)
    g = cen * lax.rsqrt(var + NORM_EPS) * glg_ref[...] + glb_ref[...]
    if not prompt:
        g_ref[...] = g

    n_grp = gm_w // GM_HEAD_DIM
    gb = g.astype(_BF16)
    wmix = wmix_ref[...]
    bmix = bmix_ref[...]
    lane_grp = lax.broadcasted_iota(jnp.int32, (CHUNK, gm_w), 1) // GM_HEAD_DIM
    pieces = []
    for c in range(tm // CHUNK):
        rows = slice(c * CHUNK, (c + 1) * CHUNK)
        stack = jnp.concatenate(
            [jnp.where(lane_grp == h, gb[rows], jnp.zeros_like(gb[rows])) for h in range(n_grp)],
            axis=0)
        mixed = _dot(wmix, stack) + bmix
        pieces.append(u[rows] * mixed)
    m = jnp.concatenate(pieces, axis=0)
    mixn_ref[...] = _rms_rows(m, gmo_ref[...]).astype(_BF16)


def _project(x2d, seq, wts, wmix, bmix, prompt, tm, tables=()):
    n, d_model = x2d.shape
    att_w = wts["qg"].shape[1]
    gm_w = wts["glg"].shape[1]
    steps = n // tm
    per_seq = seq // tm if prompt else 1
    row = lambda i: (i, 0)
    full2 = lambda i: (0, 0)
    in_specs = [
        pl.BlockSpec((tm, d_model), row),
        pl.BlockSpec((1, d_model), full2),
        pl.BlockSpec(wts["w_in"].shape, full2),
        pl.BlockSpec((1, att_w), full2),
        pl.BlockSpec((1, att_w), full2),
        pl.BlockSpec((1, gm_w), full2),
        pl.BlockSpec((1, gm_w), full2),
        pl.BlockSpec(wts["gsum"].shape, full2),
        pl.BlockSpec(wmix.shape, full2),
        pl.BlockSpec(bmix.shape, full2),
        pl.BlockSpec((1, gm_w), full2),
    ]
    f32_rows = jax.ShapeDtypeStruct((n, att_w), _F32)
    if prompt:
        batch = n // seq
        n_pair = att_w // 128
        n_heads = att_w // HEAD_DIM
        blk_per_tile = tm // MOBA_BLOCK
        nb = seq // MOBA_BLOCK
        in_specs += [
            pl.BlockSpec((n_heads, blk_per_tile, MOBA_BLOCK, 128), lambda i: (0, i % per_seq, 0, 0)),
            pl.BlockSpec((n_heads, 128, tm), lambda i: (0, 0, i % per_seq)),
        ]
        blocks = lambda i: (i // per_seq, 0, i % per_seq, 0, 0)
        kv_t = jax.ShapeDtypeStruct((batch, att_w, seq), _F32)
        kv_t_spec = pl.BlockSpec((1, att_w, tm), lambda i: (i // per_seq, 0, i % per_seq))
        out_shape = (
            kv_t, kv_t,
            jax.ShapeDtypeStruct((batch, n_pair, nb, 128, MOBA_BLOCK), _F32),
            jax.ShapeDtypeStruct((batch, n_heads, nb, 128, MOBA_BLOCK), _BF16),
            jax.ShapeDtypeStruct((batch, n_heads, nb, MOBA_BLOCK, 128), _BF16),
            jax.ShapeDtypeStruct((batch, n_heads, nb, _V_ROWS, MOBA_BLOCK), _BF16),
            jax.ShapeDtypeStruct((steps, blk_per_tile, att_w), _F32),
            jax.ShapeDtypeStruct((n, gm_w), _BF16),
        )
        out_specs = (
            kv_t_spec, kv_t_spec,
            pl.BlockSpec((1, n_pair, blk_per_tile, 128, MOBA_BLOCK), blocks),
            pl.BlockSpec((1, n_heads, blk_per_tile, 128, MOBA_BLOCK), blocks),
            pl.BlockSpec((1, n_heads, blk_per_tile, MOBA_BLOCK, 128), blocks),
            pl.BlockSpec((1, n_heads, blk_per_tile, _V_ROWS, MOBA_BLOCK), blocks),
            pl.BlockSpec((1, blk_per_tile, att_w), lambda i: (i, 0, 0)),
            pl.BlockSpec((tm, gm_w), row),
        )
    else:
        out_shape = (f32_rows, f32_rows, f32_rows,
                     jax.ShapeDtypeStruct((n, gm_w), _F32),
                     jax.ShapeDtypeStruct((n, gm_w), _BF16))
        out_specs = tuple(pl.BlockSpec((tm, att_w), row) for _ in range(5))
    return pl.pallas_call(
        functools.partial(_proj_kernel, att_w=att_w, gm_w=gm_w, prompt=prompt),
        grid=(steps,),
        in_specs=in_specs,
        out_specs=out_specs,
        out_shape=out_shape,
        compiler_params=pltpu.CompilerParams(dimension_semantics=("arbitrary",),
                                             vmem_limit_bytes=_VMEM_LIMIT),
        name="project_prompt" if prompt else "project_sample",
    )(x2d, wts["ln1_g"], wts["w_in"], wts["qg"], wts["kg"], wts["glg"], wts["glb"], wts["gsum"],
      wmix, bmix, wts["gmo"], *tables)


def _mlp_kernel(x_ref, a_ref, mixn_ref, ag_ref, wo_a_ref, wo_m_ref, ln2_ref, wup_ref, wdn_ref,
                o_ref, *, a_transposed, ff_chunk):
    if a_transposed:
        att_w = ag_ref.shape[1]
        a = jnp.concatenate([a_ref[0, :, j].reshape(att_w, MOBA_BLOCK).T
                             for j in range(a_ref.shape[2])], axis=0)
    else:
        a = a_ref[...]
    an = _rms_rows(a, ag_ref[...]).astype(_BF16)
    x1 = x_ref[...] + _dot(an, wo_a_ref[...]) + _dot(mixn_ref[...], wo_m_ref[...])
    hn = _rms_rows(x1, ln2_ref[...]).astype(_BF16)
    y = None
    for c in range(wup_ref.shape[1] // ff_chunk):
        cols = slice(c * ff_chunk, (c + 1) * ff_chunk)
        h = jnp.square(jnp.maximum(_dot(hn, wup_ref[:, cols]), 0.0)).astype(_BF16)
        d = _dot(h, wdn_ref[cols, :])
        y = d if y is None else y + d
    o_ref[...] = x1 + y


def _merge_mlp(x2d, a, mixn, wts, seq, tm, a_transposed):
    n, d_model = x2d.shape
    att_w = wts["wo_a"].shape[0]
    gm_w = wts["wo_m"].shape[0]
    d_ff = wts["w_up"].shape[1]
    row = lambda i: (i, 0)
    full2 = lambda i: (0, 0)
    once = dict(pipeline_mode=pl.Buffered(1))
    if a_transposed:
        per_seq = seq // tm
        a_spec = pl.BlockSpec((1, att_w // 128, tm // MOBA_BLOCK, 128, MOBA_BLOCK),
                              lambda i: (i // per_seq, 0, i % per_seq, 0, 0))
    else:
        a_spec = pl.BlockSpec((tm, att_w), row)
    return pl.pallas_call(
        functools.partial(_mlp_kernel, a_transposed=a_transposed, ff_chunk=min(d_ff, 1024)),
        grid=(n // tm,),
        in_specs=[
            pl.BlockSpec((tm, d_model), row),
            a_spec,
            pl.BlockSpec((tm, gm_w), row),
            pl.BlockSpec((1, att_w), full2),
            pl.BlockSpec((att_w, d_model), full2, **once),
            pl.BlockSpec((gm_w, d_model), full2, **once),
            pl.BlockSpec((1, d_model), full2),
            pl.BlockSpec((d_model, d_ff), full2, **once),
            pl.BlockSpec((d_ff, d_model), full2, **once),
        ],
        out_specs=pl.BlockSpec((tm, d_model), row),
        out_shape=jax.ShapeDtypeStruct((n, d_model), _F32),
        compiler_params=pltpu.CompilerParams(dimension_semantics=("arbitrary",),
                                             vmem_limit_bytes=_VMEM_LIMIT),
        name="merge_mlp_prompt" if a_transposed else "merge_mlp_sample",
    )(x2d, a, mixn, wts["ag"], wts["wo_a"], wts["wo_m"], wts["ln2_g"], wts["w_up"], wts["w_down"])


def _topk_rows(gate, n_valid, k):
    nb = gate.shape[0]
    rows = lax.broadcasted_iota(jnp.int32, gate.shape, 0)
    valid = rows < n_valid
    g = jnp.where(valid, gate, -jnp.inf)
    sel = jnp.zeros(gate.shape, jnp.bool_)
    for _ in range(min(k, nb)):
        best = jnp.max(g, axis=0, keepdims=True)
        first = jnp.min(jnp.where(g == best, rows, nb), axis=0, keepdims=True)
        hit = rows == first
        sel = jnp.logical_or(sel, hit)
        g = jnp.where(hit, -jnp.inf, g)
    return jnp.logical_and(sel, valid)


def _moba_prompt_kernel(qt_ref, qa_ref, ka_ref, va_ref, ksum_ref, causal_ref, o_ref,
                        sel_ref, m_ref, alpha_ref, acc_ref, t_ref, p_ref):
    nb = ka_ref.shape[2]
    tq = MOBA_BLOCK
    n_total = sum((i + 2) // 2 for i in range(nb))

    kmean = ksum_ref[0, 0] * (1.0 / MOBA_BLOCK)
    lane_head = lax.broadcasted_iota(jnp.int32, kmean.shape, 1) // HEAD_DIM
    kmean2 = jnp.concatenate([jnp.where(lane_head == r, kmean, 0.0) for r in range(2)], axis=0)

    def select(step, carry):
        for u in range(_SELECT_TILES):
            i = step * _SELECT_TILES + u
            gate = jnp.dot(kmean2, qt_ref[0, 0, i], precision=lax.Precision.HIGHEST,
                           preferred_element_type=_F32)
            for r in range(2):
                sel_ref[i, r] = _topk_rows(gate[r * nb:(r + 1) * nb], i, MOBA_TOPK).astype(_F32)
        return carry

    lax.fori_loop(0, nb // _SELECT_TILES, select, 0)

    ones_row = jnp.where(lax.broadcasted_iota(jnp.int32, (_V_ROWS, tq), 0) == HEAD_DIM, 1.0, 0.0)
    for r in range(2):
        m_ref[r] = jnp.full((1, tq), _NEG, _F32)
        acc_ref[r] = ones_row
    alpha_ref[...] = jnp.ones(alpha_ref.shape, _F32)
    p_ref[1] = jnp.zeros(p_ref.shape[1:], p_ref.dtype)

    def advance(i, k):
        last = k + 1 >= lax.div(i + 2, 2)
        return jnp.where(last, i + 1, i), jnp.where(last, 0, k + 1)

    def blocks(i, k):
        tile = jnp.minimum(i, nb - 1)
        first = jnp.where(k == 0, tile, jnp.minimum(2 * k - 1, nb - 1))
        return tile, first, jnp.minimum(2 * k, nb - 1)

    def scores(item, slot):
        tile, *ns = blocks(*item)
        for r in range(2):
            for d, n in enumerate(ns):
                t_ref[slot, r, d] = _dot(ka_ref[0, r, n], qa_ref[0, r, tile])

    def softmax(item, slot):
        i, k = item
        tile, n0, n1 = blocks(i, k)
        real = i < nb
        own = jnp.logical_and(k == 0, real)
        mask = causal_ref[own.astype(jnp.int32)]
        for r in range(2):
            t0 = t_ref[slot, r, 0] + mask
            t1 = t_ref[slot, r, 1]
            c0 = jnp.logical_or(jnp.logical_and(sel_ref[tile, r, pl.ds(n0, 1), :] > 0.5, real), own)
            c1 = jnp.logical_and(sel_ref[tile, r, pl.ds(n1, 1), :] > 0.5, real)
            m = jnp.where(k == 0, _NEG, m_ref[r])
            m_new = jnp.maximum(m, jnp.maximum(
                jnp.where(c0, jnp.max(t0, axis=0, keepdims=True), _NEG),
                jnp.where(c1, jnp.max(t1, axis=0, keepdims=True), _NEG)))
            p_ref[slot, r, 0] = jnp.exp2(t0 - jnp.where(c0, m_new, -_NEG)).astype(_BF16)
            p_ref[slot, r, 1] = jnp.exp2(t1 - jnp.where(c1, m_new, -_NEG)).astype(_BF16)
            alpha_ref[slot, r] = jnp.where(real, jnp.exp2(m - m_new), 1.0)
            m_ref[r] = m_new

    def weighted_values(item, slot):
        tile, n0, n1 = blocks(*item)
        for r in range(2):
            acc = (alpha_ref[slot, r] * acc_ref[r]
                   + _dot(va_ref[0, r, n0], p_ref[slot, r, 0])
                   + _dot(va_ref[0, r, n1], p_ref[slot, r, 1]))
            acc_ref[r] = acc
            o_ref[0, 0, tile, r * HEAD_DIM:(r + 1) * HEAD_DIM, :] = (
                acc[0:HEAD_DIM] / acc[HEAD_DIM:HEAD_DIM + 1])

    def step(_, carry):
        before, item = carry
        for _ in range(_ITEM_PAIRS):
            after = advance(*item)
            after2 = advance(*after)
            weighted_values(before, 1)
            softmax(item, 0)
            scores(after, 1)
            weighted_values(item, 0)
            scores(after2, 0)
            softmax(after, 1)
            before, item = after, after2
        return before, item

    first = (jnp.int32(0), jnp.int32(0))
    nothing = (jnp.int32(nb), jnp.int32(1))
    scores(first, 0)
    last, _ = lax.fori_loop(0, pl.cdiv(n_total, 2 * _ITEM_PAIRS), step, (nothing, first))
    weighted_values(last, 1)


def _moba_prompt(qt, qa, ka, va, ksum, causal):
    batch, n_pair, nb = qt.shape[:3]
    assert nb % _SELECT_TILES == 0
    tq = MOBA_BLOCK
    per_pair = lambda b, p: (b, p, 0, 0, 0)
    return pl.pallas_call(
        _moba_prompt_kernel,
        grid=(batch, n_pair),
        in_specs=[
            pl.BlockSpec((1, 1, nb, 128, tq), per_pair),
            pl.BlockSpec((1, 2, nb, 128, tq), per_pair),
            pl.BlockSpec((1, 2, nb, MOBA_BLOCK, 128), per_pair),
            pl.BlockSpec((1, 2, nb, _V_ROWS, MOBA_BLOCK), per_pair),
            pl.BlockSpec((1, 1, nb, 128), lambda b, p: (b, p, 0, 0)),
            pl.BlockSpec((2, MOBA_BLOCK, tq), lambda b, p: (0, 0, 0)),
        ],
        out_specs=pl.BlockSpec((1, 1, nb, 128, tq), per_pair),
        out_shape=jax.ShapeDtypeStruct((batch, n_pair, nb, 128, tq), _F32),
        scratch_shapes=[pltpu.VMEM((nb, 2, nb, tq), _F32),
                        pltpu.VMEM((2, 1, tq), _F32),
                        pltpu.VMEM((2, 2, 1, tq), _F32),
                        pltpu.VMEM((2, _V_ROWS, tq), _F32),
                        pltpu.VMEM((2, 2, 2, MOBA_BLOCK, tq), _F32),
                        pltpu.VMEM((2, 2, 2, MOBA_BLOCK, tq), _BF16)],
        compiler_params=pltpu.CompilerParams(
            dimension_semantics=("arbitrary", "arbitrary"),
            vmem_limit_bytes=_VMEM_LIMIT),
        name="moba_prompt",
    )(qt, qa, ka, va, ksum, causal)


_RING = 64
_AHEAD = 56
_PAGE_GROUP = 8


def _moba_sample_kernel(pt_ref, q_ref, kn_ref, vn_ref, alibi_ref, newbias_ref,
                        ck_ref, cv_ref, o_ref, kring_ref, vring_ref, ksem_ref, vsem_ref,
                        s_ref, *, n_heads):
    b = pl.program_id(0)
    n_seq = pl.num_programs(0)
    n_pages = pt_ref.shape[1]
    tn, att_w = q_ref.shape
    q_rows = n_heads * tn
    ppb = MOBA_BLOCK // PAGE_SIZE
    nb = n_pages // ppb
    i0 = b * n_pages

    def copy(src_ref, ring_ref, sem_ref, page, slot):
        return pltpu.make_async_copy(src_ref.at[page], ring_ref.at[slot], sem_ref.at[slot])

    def fetch(src_ref, ring_ref, sem_ref, i):
        seq = jnp.minimum(i // n_pages, n_seq - 1)
        copy(src_ref, ring_ref, sem_ref, pt_ref[seq, i % n_pages], i % _RING).start(
            priority=int(src_ref is cv_ref))

    def wait(src_ref, ring_ref, sem_ref, i):
        copy(src_ref, ring_ref, sem_ref, 0, i % _RING).wait()

    k_stream = (ck_ref, kring_ref, ksem_ref)
    v_stream = (cv_ref, vring_ref, vsem_ref)

    @pl.when(b == 0)
    def _():
        for i in range(_AHEAD):
            fetch(*k_stream, jnp.int32(i))
            fetch(*v_stream, jnp.int32(i))

    q = q_ref[...]
    qrow = lax.broadcasted_iota(jnp.int32, (q_rows, att_w), 0)
    qlane = lax.broadcasted_iota(jnp.int32, (q_rows, att_w), 1)
    qbd = jnp.where(qrow // tn == qlane // HEAD_DIM, jnp.concatenate([q] * n_heads, axis=0), 0.0)
    qbd = (qbd * HEAD_DIM ** -0.5).astype(_BF16)
    lane = lax.broadcasted_iota(jnp.int32, (q_rows, 128), 1)

    def k_pages(step, gate):
        first = step * _PAGE_GROUP
        for u in range(_PAGE_GROUP):
            wait(*k_stream, i0 + first + u)
        for u in range(_PAGE_GROUP):
            slot = (i0 + first + u) % _RING
            s = _dot(qbd, kring_ref[slot].astype(_BF16))
            s_ref[first + u] = s
            gate = gate + jnp.where(lane == (first + u) // ppb,
                                    jnp.sum(s, axis=1, keepdims=True), 0.0)
        for u in range(_PAGE_GROUP):
            fetch(*k_stream, i0 + first + u + _AHEAD)
        return gate

    gate = lax.fori_loop(0, n_pages // _PAGE_GROUP, k_pages, jnp.zeros((q_rows, 128), _F32))

    gate = jnp.where(lane < nb, gate, -jnp.inf)
    sel = jnp.zeros(gate.shape, jnp.bool_)
    for _ in range(min(MOBA_TOPK, nb)):
        best = jnp.max(gate, axis=1, keepdims=True)
        first = jnp.min(jnp.where(gate == best, lane, 128), axis=1, keepdims=True)
        hit = lane == first
        sel = jnp.logical_or(sel, hit)
        gate = jnp.where(hit, -jnp.inf, gate)
    sel = jnp.where(jnp.logical_and(sel, lane < nb), 1.0, 0.0)

    slope = alibi_ref[0]
    origin = alibi_ref[1]

    def logits_pages(step, run_max):
        for u in range(_PAGE_GROUP):
            j = step * _PAGE_GROUP + u
            pos = (lane + j * PAGE_SIZE).astype(_F32)
            chosen = jnp.sum(jnp.where(lane == j // ppb, sel, 0.0), axis=1, keepdims=True) > 0.5
            logit = jnp.where(chosen, s_ref[j] + (slope * pos + origin), _NEG)
            s_ref[j] = logit
            run_max = jnp.maximum(run_max, logit)
        return run_max

    run_max = lax.fori_loop(0, n_pages // _PAGE_GROUP, logits_pages,
                            jnp.full((q_rows, 128), _NEG, _F32))
    logit_new = lax.dot_general(qbd, kn_ref[...].astype(_BF16), _NT,
                                preferred_element_type=_F32) + newbias_ref[...]
    m = jnp.maximum(jnp.max(run_max, axis=1, keepdims=True), jnp.max(logit_new, axis=1, keepdims=True))
    m_tile = jnp.broadcast_to(m, (q_rows, 128))

    def exp_pages(step, run_sum):
        for u in range(_PAGE_GROUP):
            j = step * _PAGE_GROUP + u
            p = jnp.exp(s_ref[j] - m_tile)
            s_ref[j] = p
            run_sum = run_sum + p
        return run_sum

    run_sum = lax.fori_loop(0, n_pages // _PAGE_GROUP, exp_pages, jnp.zeros((q_rows, 128), _F32))
    p_new = jnp.exp(logit_new - m)
    inv = 1.0 / (jnp.sum(run_sum, axis=1, keepdims=True) + jnp.sum(p_new, axis=1, keepdims=True))
    inv_tile = jnp.broadcast_to(inv, (q_rows, 128))

    def v_pages(step, acc):
        first = step * _PAGE_GROUP
        for u in range(_PAGE_GROUP):
            wait(*v_stream, i0 + first + u)
        for u in range(_PAGE_GROUP):
            slot = (i0 + first + u) % _RING
            p = (s_ref[first + u] * inv_tile).astype(_BF16)
            acc = acc + lax.dot_general(p, vring_ref[slot].astype(_BF16), _NT,
                                        preferred_element_type=_F32)
        for u in range(_PAGE_GROUP):
            fetch(*v_stream, i0 + first + u + _AHEAD)
        return acc

    acc = _dot((p_new * inv).astype(_BF16), vn_ref[...].astype(_BF16))
    acc = lax.fori_loop(0, n_pages // _PAGE_GROUP, v_pages, acc)

    @pl.when(b == n_seq - 1)
    def _():
        for i in range(_AHEAD):
            wait(*k_stream, n_seq * n_pages + i)
            wait(*v_stream, n_seq * n_pages + i)

    lane_head = lax.broadcasted_iota(jnp.int32, (tn, att_w), 1) // HEAD_DIM
    out = jnp.zeros((tn, att_w), _F32)
    for h in range(n_heads):
        out = out + jnp.where(lane_head == h, acc[h * tn:(h + 1) * tn, :], 0.0)
    o_ref[...] = out


def _sample_tables(slopes, tn, n_pages):
    n_heads = slopes.shape[0]
    n_keys = n_pages * PAGE_SIZE
    row = jnp.arange(n_heads * tn)
    slope_row = slopes[row // tn][:, None]
    tok = (row % tn).astype(_F32)[:, None]
    alibi = jnp.stack([jnp.broadcast_to(slope_row, (n_heads * tn, 128)),
                       jnp.broadcast_to(-slope_row * (float(n_keys) + tok), (n_heads * tn, 128))])
    dist = tok - jnp.arange(tn, dtype=_F32)[None, :]
    newbias = jnp.where(dist >= 0.0, -slope_row * dist, _NEG)
    return alibi, newbias


def _moba_sample(q, k_new, v_new, cache_k, cache_v, pages, tables, tn):
    n, att_w = q.shape
    n_heads = att_w // HEAD_DIM
    n_seq, n_pages = pages.shape
    assert n_pages % _PAGE_GROUP == 0 and _AHEAD + _PAGE_GROUP <= _RING
    row = lambda b, pt: (b, 0)
    return pl.pallas_call(
        functools.partial(_moba_sample_kernel, n_heads=n_heads),
        grid_spec=pltpu.PrefetchScalarGridSpec(
            num_scalar_prefetch=1,
            grid=(n_seq,),
            in_specs=[
                pl.BlockSpec((tn, att_w), row),
                pl.BlockSpec((tn, att_w), row),
                pl.BlockSpec((tn, att_w), row),
                *(pl.BlockSpec(t.shape, lambda b, pt, nd=t.ndim: (0,) * nd) for t in tables),
                pl.BlockSpec(memory_space=pl.ANY),
                pl.BlockSpec(memory_space=pl.ANY),
            ],
            out_specs=pl.BlockSpec((tn, att_w), row),
            scratch_shapes=[
                pltpu.VMEM((_RING, att_w, PAGE_SIZE), _F32),
                pltpu.VMEM((_RING, att_w, PAGE_SIZE), _F32),
                pltpu.SemaphoreType.DMA((_RING,)),
                pltpu.SemaphoreType.DMA((_RING,)),
                pltpu.VMEM((n_pages, n_heads * tn, PAGE_SIZE), _F32),
            ],
        ),
        out_shape=jax.ShapeDtypeStruct((n, att_w), _F32),
        compiler_params=pltpu.CompilerParams(dimension_semantics=("arbitrary",),
                                             vmem_limit_bytes=_VMEM_LIMIT),
        name="moba_sample",
    )(pages, q, k_new, v_new, *tables, cache_k, cache_v)


def _alibi_slopes(n_heads):
    h = jnp.arange(1, n_heads + 1, dtype=_F32)
    return jnp.exp2(-8.0 * h / n_heads)


def _split_terms(x):
    terms = []
    for _ in range(_POS_TERMS):
        bits = lax.bitcast_convert_type(x, jnp.uint32) & jnp.uint32(0xFFFF0000)
        t = lax.bitcast_convert_type(bits, _F32)
        terms.append(t.astype(_BF16))
        x = x - t
    return terms


def _position_tables(slopes, seq):
    n_heads = slopes.shape[0]
    pos = jnp.arange(seq, dtype=_F32)
    val = (slopes * _LOG2E)[:, None] * pos[None, :]
    start = HEAD_DIM * (1 - jnp.arange(n_heads) % 2)
    slot = jnp.arange(128)[None, :] - start[:, None]

    def table(terms, first, slot):
        out = jnp.where(jnp.logical_and(slot >= 0, slot < 2 * _POS_TERMS), 1.0, 0.0)
        for n, term in enumerate(terms):
            out = jnp.where(slot == first + n, term.astype(_F32), out)
        return out.astype(_BF16)

    kpos = table([t[:, :, None] for t in _split_terms(val)], 0, slot[:, None, :])
    qpos = table([t[:, None, :] for t in _split_terms(-val)], _POS_TERMS, slot[:, :, None])
    return kpos.reshape(n_heads, seq // MOBA_BLOCK, MOBA_BLOCK, 128), qpos


def kernel(x_prompt, x_sample, cache_k, cache_v, page_table, ln1_g, w_in, q_norm_g, k_norm_g,
           gm_ln_g, gm_ln_b, w_s, b_s, att_out_g, gm_out_g, w_out, ln2_g, w_up, w_down):
    batch, seq, d_model = x_prompt.shape
    dec_batch, tn, _ = x_sample.shape
    depth, n_pool, page_size, n_heads, head_dim = cache_k.shape
    gm_heads, gm_dim = gm_ln_g.shape[1:]
    att_w = n_heads * head_dim
    gm_w = gm_heads * gm_dim
    n_pages = page_table.shape[1]
    assert (head_dim, gm_dim, page_size, w_s.shape[-1]) == (HEAD_DIM, GM_HEAD_DIM, PAGE_SIZE, CHUNK)
    assert att_w == gm_w and att_w % 128 == 0 and n_heads * tn <= 128 and 128 % n_heads == 0
    assert seq % MOBA_BLOCK == 0 and CHUNK % tn == 0
    assert (n_pages * PAGE_SIZE) % MOBA_BLOCK == 0 and n_pages * PAGE_SIZE >= MOBA_BLOCK
    tm = min(_ROW_TILE, seq)
    tm_s = min(_ROW_TILE, dec_batch * tn)
    assert seq % tm == 0 and tm % MOBA_BLOCK == 0 and (dec_batch * tn) % tm_s == 0 and tm_s % CHUNK == 0
    n_pair = att_w // 128
    nb = seq // MOBA_BLOCK

    slopes = _alibi_slopes(n_heads)
    kpos, qpos = _position_tables(slopes, seq)
    group = jnp.arange(att_w) // HEAD_DIM
    gsum = (group[:, None] == group[None, :]).astype(_BF16)
    sample_tables = _sample_tables(slopes, tn, n_pages)
    rel = jnp.arange(MOBA_BLOCK)
    causal = jnp.stack([jnp.zeros((MOBA_BLOCK, MOBA_BLOCK), _F32),
                        jnp.where(rel[:, None] <= rel[None, :], 0.0, _NEG)])

    xp = x_prompt.reshape(batch * seq, d_model)
    xs = x_sample.reshape(dec_batch * tn, d_model)
    ck = cache_k.transpose(0, 1, 3, 4, 2).reshape(depth * n_pool, att_w, page_size)
    cv = cache_v.transpose(0, 1, 3, 4, 2).reshape(depth * n_pool, att_w, page_size)
    outs = [[] for _ in range(5)]
    for l in range(depth):
        wts = dict(
            ln1_g=ln1_g[l][None], w_in=w_in[l].astype(_BF16),
            qg=q_norm_g[l].reshape(1, att_w), kg=k_norm_g[l].reshape(1, att_w),
            glg=gm_ln_g[l].reshape(1, gm_w), glb=gm_ln_b[l].reshape(1, gm_w),
            gsum=gsum, gmo=gm_out_g[l][None], ag=att_out_g[l][None],
            wo_a=w_out[l][:att_w].astype(_BF16), wo_m=w_out[l][att_w:].astype(_BF16),
            ln2_g=ln2_g[l][None], w_up=w_up[l].astype(_BF16), w_down=w_down[l].astype(_BF16))
        w_tril = jnp.tril(w_s[l])
        wmix_p = w_tril.transpose(1, 0, 2).reshape(CHUNK, gm_heads * CHUNK).astype(_BF16)
        bmix_p = jnp.repeat(b_s[l].T, GM_HEAD_DIM, axis=1)
        eye = jnp.eye(CHUNK // tn, dtype=_F32)
        w_blk = jnp.einsum("ab,hij->haibj", eye, w_tril[:, :tn, :tn]).reshape(gm_heads, CHUNK, CHUNK)
        wmix_s = w_blk.transpose(1, 0, 2).reshape(CHUNK, gm_heads * CHUNK).astype(_BF16)
        bmix_s = jnp.tile(jnp.repeat(b_s[l][:, :tn].T, GM_HEAD_DIM, axis=1), (CHUNK // tn, 1))

        k_p, v_p, qt, qa, ka, va, ksum, mixn_p = _project(xp, seq, wts, wmix_p, bmix_p, True, tm,
                                                          (kpos, qpos))
        ksum = ksum.reshape(batch, nb, n_pair, 128).transpose(0, 2, 1, 3)
        a_t = _moba_prompt(qt, qa, ka, va, ksum, causal)
        xp = _merge_mlp(xp, a_t, mixn_p, wts, seq, tm, True)

        k_s, v_s, q_s, g_s, mixn_s = _project(xs, tn, wts, wmix_s, bmix_s, False, tm_s)
        a_s = _moba_sample(q_s, k_s, v_s, ck, cv, page_table + l * n_pool, sample_tables, tn)
        xs = _merge_mlp(xs, a_s, mixn_s, wts, tn, tm_s, False)

        outs[0].append(k_p.reshape(batch, n_heads, head_dim, seq).transpose(0, 3, 1, 2))
        outs[1].append(v_p.reshape(batch, n_heads, head_dim, seq).transpose(0, 3, 1, 2))
        outs[2].append(k_s.reshape(dec_batch, tn, n_heads, head_dim))
        outs[3].append(v_s.reshape(dec_batch, tn, n_heads, head_dim))
        outs[4].append(g_s.reshape(dec_batch, tn, gm_heads, gm_dim))
    return (xp.reshape(batch, seq, d_model), xs.reshape(dec_batch, tn, d_model),
            *(jnp.stack(o) for o in outs))
```
